```python
import math
import jax, jax.numpy as jnp
from jax import lax
import numpy as np

D_MODEL = 2048
BATCH = 2
SEQ = 16384
DEPTH = 4

D_MIX = D_MODEL
GROUP_W = D_MIX // 4
D_FF = 5632
NORM_EPS = 1e-6
MASK_VALUE = -1e30
HG_HEADS = 4
HG_EXPAND = GROUP_W // HG_HEADS
HG_HEAD_V = GROUP_W // HG_HEADS
HG_CHUNK = 64
ATT_HEADS = 8
ATT_HEAD_DIM = GROUP_W // ATT_HEADS
ATT_PATTERNS = ((128, 1), (512, 4), (2048, 16))
ATT_BLOCK = 128
SSM_HEAD_DIM = 64
SSM_HEADS = GROUP_W // SSM_HEAD_DIM
SSM_GROUPS = 2
SSM_STATE = 128
SSM_CONV = 4
SSM_CHUNK = 64
SSM_CONV_DIM = GROUP_W + 2 * SSM_GROUPS * SSM_STATE
LRU_BLOCKS = 8
LRU_BLOCK_W = GROUP_W // LRU_BLOCKS
LRU_CONV = 4
LRU_C = 8.0
IN_WIDTHS = (GROUP_W, GROUP_W, GROUP_W, GROUP_W,
             GROUP_W, GROUP_W, GROUP_W,
             GROUP_W, SSM_CONV_DIM, SSM_HEADS,
             GROUP_W, GROUP_W)
N_IN = sum(IN_WIDTHS)

kernel_name = "hybrid_parallel_hgrn2_dilattn_ssd_rglru"


def rmsnorm(x, w):
    xf = x.astype(jnp.float32)
    y = xf * lax.rsqrt(jnp.mean(xf * xf, axis=-1, keepdims=True) + NORM_EPS)
    return (y * w.astype(jnp.float32)).astype(x.dtype)


def swiglu(x, w_gate, w_up, w_down):
    return (jax.nn.silu(x @ w_gate) * (x @ w_up)) @ w_down


def causal_dwconv(x, w, b):
    width, ch = w.shape
    y = lax.conv_general_dilated(x, w[:, None, :].astype(x.dtype), window_strides=(1,),
                                 padding=[(width - 1, 0)],
                                 dimension_numbers=("NWC", "WIO", "NWC"),
                                 feature_group_count=ch)
    return y + b.astype(x.dtype)


def hgrn2_chunk_scan(q, k, v, log_f):
    bsz, seq, heads, dk = q.shape
    dv = v.shape[-1]
    c = HG_CHUNK
    nc = seq // c

    def to_chunks(t):
        return t.reshape(bsz, nc, c, heads, t.shape[-1]).transpose(1, 0, 3, 2, 4)

    mask = jnp.tril(jnp.ones((c, c), dtype=bool))[:, :, None]

    def step(state, inp):
        qc, kc, vc, gc = inp
        b = jnp.cumsum(gc, axis=2)
        o_inter = jnp.einsum("bhtk,bhkv->bhtv", qc * jnp.exp(b), state)
        diff = b[:, :, :, None, :] - b[:, :, None, :, :]
        decay = jnp.where(mask, jnp.exp(jnp.where(mask, diff, 0.0)), 0.0)
        scores = jnp.einsum("bhtk,bhsk,bhtsk->bhts", qc, kc, decay)
        o = o_inter + jnp.einsum("bhts,bhsv->bhtv", scores, vc)
        b_last = b[:, :, -1:, :]
        state = (jnp.exp(b_last[:, :, 0, :])[..., None] * state
                 + jnp.einsum("bhsk,bhsv->bhkv", kc * jnp.exp(b_last - b), vc))
        return state, o

    init = jnp.zeros((bsz, heads, dk, dv), jnp.float32)
    _, o = lax.scan(step, init, (to_chunks(q), to_chunks(k), to_chunks(v), to_chunks(log_f)))
    return o.transpose(1, 0, 3, 2, 4).reshape(bsz, seq, heads, dv)


def hgrn2_mixer(q_raw, f_raw, i_raw, g_raw, lower_bound, norm_w):
    bsz, seq, _ = q_raw.shape
    out_dtype = q_raw.dtype

    def heads(t):
        return t.astype(jnp.float32).reshape(bsz, seq, HG_HEADS, -1)

    q = jax.nn.silu(heads(q_raw))
    z = heads(f_raw)
    lb = lower_bound.astype(jnp.float32).reshape(HG_HEADS, HG_EXPAND)
    f = lb + (1.0 - lb) * jax.nn.sigmoid(z)
    log_f = jnp.log(f)
    k = (1.0 - lb) * jax.nn.sigmoid(-z)
    v = heads(i_raw)
    o = hgrn2_chunk_scan(q, k, v, log_f)
    o = rmsnorm(o, norm_w.reshape(HG_HEADS, HG_HEAD_V)) * jax.nn.silu(heads(g_raw))
    return o.reshape(bsz, seq, GROUP_W).astype(out_dtype)


def dilated_pattern(q, k, v, dil, span):
    bsz, seq, heads, hd = q.shape
    qb_len = ATT_BLOCK
    unit = dil * qb_len
    seq_p = -(-seq // unit) * unit
    m_len = seq_p // dil
    nb = m_len // qb_len
    pad = ((0, 0), (0, seq_p - seq), (0, 0), (0, 0))

    def to_blocks(t):
        return (jnp.pad(t, pad).reshape(bsz, m_len, dil, heads, hd)
                .transpose(0, 2, 1, 3, 4).reshape(bsz, dil, nb, qb_len, heads, hd))

    def with_prev(t):
        prev = jnp.pad(t[:, :, :-1], ((0, 0), (0, 0), (1, 0), (0, 0), (0, 0), (0, 0)))
        return jnp.concatenate([prev, t], axis=3)

    qb = to_blocks(q)
    kk = with_prev(to_blocks(k))
    vv = with_prev(to_blocks(v))
    s = jnp.einsum("brnqhe,brnkhe->brnhqk", qb, kk) * (hd ** -0.5)
    iq = jnp.arange(qb_len)[:, None]
    ik = jnp.arange(2 * qb_len)[None, :]
    rel = qb_len + iq - ik
    blk = jnp.arange(nb)[:, None, None]
    valid = ((rel >= 0) & (rel <= span) & (blk * qb_len + ik[None] - qb_len >= 0))[:, None]
    s = jnp.where(valid, s, MASK_VALUE)
    m = jnp.max(s, axis=-1)
    p = jnp.where(valid, jnp.exp(s - m[..., None]), 0.0)
    den = jnp.sum(p, axis=-1)
    num = jnp.einsum("brnhqk,brnkhe->brnqhe", p, vv)
    num = (num.reshape(bsz, dil, m_len, heads, hd).transpose(0, 2, 1, 3, 4)
           .reshape(bsz, seq_p, heads, hd)[:, :seq])

    def stat_back(t):
        return (t.transpose(0, 1, 2, 4, 3).reshape(bsz, dil, m_len, heads)
                .transpose(0, 2, 1, 3).reshape(bsz, seq_p, heads)[:, :seq])

    return num, stat_back(m), stat_back(den)


def dilated_attention(q_raw, k_raw, v_raw, norm_w):
    bsz, seq, _ = q_raw.shape

    def heads(t):
        return t.astype(jnp.float32).reshape(bsz, seq, ATT_HEADS, ATT_HEAD_DIM)

    q, k, v = heads(q_raw), heads(k_raw), heads(v_raw)
    outs = [dilated_pattern(q, k, v, dil, win // dil) for (win, dil) in ATT_PATTERNS]
    nums = jnp.stack([o[0] for o in outs])
    ms = jnp.stack([o[1] for o in outs])
    dens = jnp.stack([o[2] for o in outs])
    wts = jnp.exp(ms - jnp.max(ms, axis=0))
    o = jnp.sum(wts[..., None] * nums, axis=0) / jnp.sum(wts * dens, axis=0)[..., None]
    return rmsnorm(o.reshape(bsz, seq, GROUP_W), norm_w).astype(q_raw.dtype)


def ssd_chunked(xdt, adt, bm, cm):
    bsz, seq, heads, hp = xdt.shape
    lc = SSM_CHUNK
    nc = seq // lc
    g = SSM_GROUPS
    r = heads // g
    xc = xdt.reshape(bsz, nc, lc, g, r, hp)
    a = adt.reshape(bsz, nc, lc, g, r).transpose(0, 3, 4, 1, 2)
    bc = bm.reshape(bsz, nc, lc, g, SSM_STATE)
    cc = cm.reshape(bsz, nc, lc, g, SSM_STATE)
    a_cs = jnp.cumsum(a, axis=-1)
    mask = jnp.tril(jnp.ones((lc, lc), dtype=bool))
    seg = a_cs[..., :, None] - a_cs[..., None, :]
    lmat = jnp.where(mask, jnp.exp(jnp.where(mask, seg, 0.0)), 0.0)
    cb = jnp.einsum("bclgn,bcsgn->bgcls", cc, bc)
    y_diag = jnp.einsum("bgcls,bgrcls,bcsgrp->bclgrp", cb, lmat, xc)
    decay_states = jnp.exp(a_cs[..., -1:] - a_cs)
    states = jnp.einsum("bcsgn,bgrcs,bcsgrp->bcgrpn", bc, decay_states, xc)
    chunk_decay = jnp.exp(a_cs[..., -1])

    def step(carry, inp):
        st, dec = inp
        return dec[..., None, None] * carry + st, carry

    init = jnp.zeros((bsz, g, r, hp, SSM_STATE), jnp.float32)
    _, prev = lax.scan(step, init, (states.transpose(1, 0, 2, 3, 4, 5),
                                    chunk_decay.transpose(3, 0, 1, 2)))
    prev = prev.transpose(1, 0, 2, 3, 4, 5)
    y_off = jnp.einsum("bclgn,bcgrpn,bgrcl->bclgrp", cc, prev, jnp.exp(a_cs))
    return (y_diag + y_off).reshape(bsz, seq, heads, hp)


def mamba2_mixer(z, xbc, dt_raw, conv_w, conv_b, dt_bias, a_log, d_skip, norm_w):
    bsz, seq, _ = z.shape
    xbc = jax.nn.silu(causal_dwconv(xbc, conv_w, conv_b)).astype(jnp.float32)
    xs, bm, cm = jnp.split(xbc, [GROUP_W, GROUP_W + SSM_GROUPS * SSM_STATE], axis=-1)
    x = xs.reshape(bsz, seq, SSM_HEADS, SSM_HEAD_DIM)
    bm = bm.reshape(bsz, seq, SSM_GROUPS, SSM_STATE)
    cm = cm.reshape(bsz, seq, SSM_GROUPS, SSM_STATE)
    dt = jax.nn.softplus(dt_raw.astype(jnp.float32) + dt_bias.astype(jnp.float32))
    a = -jnp.exp(a_log.astype(jnp.float32))
    y = ssd_chunked(x * dt[..., None], dt * a, bm, cm)
    y = y + d_skip.astype(jnp.float32)[:, None] * x
    y = y.reshape(bsz, seq, GROUP_W) * jax.nn.silu(z.astype(jnp.float32))
    y = rmsnorm(y.reshape(bsz, seq, SSM_GROUPS, GROUP_W // SSM_GROUPS),
                norm_w.reshape(SSM_GROUPS, GROUP_W // SSM_GROUPS))
    return y.reshape(bsz, seq, GROUP_W).astype(z.dtype)


def _lin_combine(e1, e2):
    a1, b1 = e1
    a2, b2 = e2
    return a1 * a2, a2 * b1 + b2


def rglru_mixer(xb, gb, conv_w, conv_b, w_a, b_a, w_x, b_x, a_param, norm_w):
    bsz, seq, _ = xb.shape
    xc = causal_dwconv(xb, conv_w, conv_b).astype(jnp.float32)
    xh = xc.reshape(bsz, seq, LRU_BLOCKS, LRU_BLOCK_W)
    r = jax.nn.sigmoid(jnp.einsum("bshi,hij->bshj", xh, w_a.astype(jnp.float32)) + b_a.astype(jnp.float32))
    i = jax.nn.sigmoid(jnp.einsum("bshi,hij->bshj", xh, w_x.astype(jnp.float32)) + b_x.astype(jnp.float32))
    log_a = -LRU_C * r * jax.nn.softplus(-a_param.astype(jnp.float32).reshape(LRU_BLOCKS, LRU_BLOCK_W))
    a = jnp.exp(log_a)
    bterm = jnp.sqrt(jnp.maximum(-jnp.expm1(2.0 * log_a), 0.0)) * (i * xh)
    _, h = lax.associative_scan(_lin_combine, (a, bterm), axis=1)
    h = h.reshape(bsz, seq, GROUP_W) * jax.nn.gelu(gb.astype(jnp.float32))
    return rmsnorm(h, norm_w).astype(xb.dtype)


def setup_inputs(seed: int = 0) -> dict:
    key = jax.random.key(seed)
    ks = iter(jax.random.split(key, 40))
    f32 = jnp.float32

    def nrm(shape, scale):
        return jax.random.normal(next(ks), shape, f32) * scale

    def gain(shape):
        return 1.0 + nrm(shape, 0.02)

    L = DEPTH
    x = nrm((BATCH, SEQ, D_MODEL), 1.0)
    ffn1_norm = gain((L, D_MODEL))
    ffn1_w_gate = nrm((L, D_MODEL, D_FF), D_MODEL ** -0.5)
    ffn1_w_up = nrm((L, D_MODEL, D_FF), D_MODEL ** -0.5)
    ffn1_w_down = nrm((L, D_FF, D_MODEL), D_FF ** -0.5)
    mix_norm = gain((L, D_MODEL))
    w_in = nrm((L, D_MODEL, N_IN), D_MODEL ** -0.5)
    w_out = nrm((L, D_MIX, D_MODEL), D_MIX ** -0.5)
    hgrn_lb_logits = nrm((L, GROUP_W), 0.5)
    hgrn_norm = gain((L, GROUP_W))
    attn_norm = gain((L, GROUP_W))
    ssm_conv_w = nrm((L, SSM_CONV, SSM_CONV_DIM), SSM_CONV ** -0.5)
    ssm_conv_b = nrm((L, SSM_CONV_DIM), 0.01)
    dt0 = jnp.exp(jax.random.uniform(next(ks), (L, SSM_HEADS), f32,
                                     minval=math.log(1e-3), maxval=math.log(1e-1)))
    ssm_dt_bias = dt0 + jnp.log(-jnp.expm1(-dt0))
    ssm_a_log = jnp.log(jax.random.uniform(next(ks), (L, SSM_HEADS), f32, minval=1.0, maxval=16.0))
    ssm_d = gain((L, SSM_HEADS))
    ssm_norm = gain((L, GROUP_W))
    lru_conv_w = nrm((L, LRU_CONV, GROUP_W), LRU_CONV ** -0.5)
    lru_conv_b = nrm((L, GROUP_W), 0.01)
    lru_w_a = nrm((L, LRU_BLOCKS, LRU_BLOCK_W, LRU_BLOCK_W), LRU_BLOCK_W ** -0.5)
    lru_b_a = nrm((L, LRU_BLOCKS, LRU_BLOCK_W), 0.01)
    lru_w_x = nrm((L, LRU_BLOCKS, LRU_BLOCK_W, LRU_BLOCK_W), LRU_BLOCK_W ** -0.5)
    lru_b_x = nrm((L, LRU_BLOCKS, LRU_BLOCK_W), 0.01)
    a_c = jax.random.uniform(next(ks), (L, GROUP_W), f32, minval=0.9, maxval=0.999)
    s = a_c ** (1.0 / LRU_C)
    lru_a_param = jnp.log(s) - jnp.log1p(-s)
    lru_norm = gain((L, GROUP_W))
    ffn2_norm = gain((L, D_MODEL))
    ffn2_w_gate = nrm((L, D_MODEL, D_FF), D_MODEL ** -0.5)
    ffn2_w_up = nrm((L, D_MODEL, D_FF), D_MODEL ** -0.5)
    ffn2_w_down = nrm((L, D_FF, D_MODEL), D_FF ** -0.5)
    final_norm = gain((D_MODEL,))
    return {"x": x, "ffn1_norm": ffn1_norm, "ffn1_w_gate": ffn1_w_gate, "ffn1_w_up": ffn1_w_up,
            "ffn1_w_down": ffn1_w_down, "mix_norm": mix_norm, "w_in": w_in, "w_out": w_out,
            "hgrn_lb_logits": hgrn_lb_logits, "hgrn_norm": hgrn_norm, "attn_norm": attn_norm,
            "ssm_conv_w": ssm_conv_w, "ssm_conv_b": ssm_conv_b, "ssm_dt_bias": ssm_dt_bias,
            "ssm_a_log": ssm_a_log, "ssm_d": ssm_d, "ssm_norm": ssm_norm,
            "lru_conv_w": lru_conv_w, "lru_conv_b": lru_conv_b, "lru_w_a": lru_w_a, "lru_b_a": lru_b_a,
            "lru_w_x": lru_w_x, "lru_b_x": lru_b_x, "lru_a_param": lru_a_param, "lru_norm": lru_norm,
            "ffn2_norm": ffn2_norm, "ffn2_w_gate": ffn2_w_gate, "ffn2_w_up": ffn2_w_up,
            "ffn2_w_down": ffn2_w_down, "final_norm": final_norm}


def reference(x, ffn1_norm, ffn1_w_gate, ffn1_w_up, ffn1_w_down, mix_norm, w_in, w_out,
              hgrn_lb_logits, hgrn_norm, attn_norm, ssm_conv_w, ssm_conv_b, ssm_dt_bias,
              ssm_a_log, ssm_d, ssm_norm, lru_conv_w, lru_conv_b, lru_w_a, lru_b_a,
              lru_w_x, lru_b_x, lru_a_param, lru_norm, ffn2_norm, ffn2_w_gate, ffn2_w_up,
              ffn2_w_down, final_norm):
    split_points = np.cumsum(np.array(IN_WIDTHS))[:-1].tolist()
    lb_logits = hgrn_lb_logits.astype(jnp.float32)
    lb_e = jnp.exp(lb_logits - jnp.max(lb_logits, axis=0, keepdims=True))
    lb_p = lb_e / jnp.sum(lb_e, axis=0, keepdims=True)
    lower_bounds = jnp.cumsum(lb_p, axis=0) - lb_p[0]
    for l in range(DEPTH):
        x = x + 0.5 * swiglu(rmsnorm(x, ffn1_norm[l]), ffn1_w_gate[l], ffn1_w_up[l], ffn1_w_down[l])
        h = rmsnorm(x, mix_norm[l])
        proj = h @ w_in[l]
        (a_q, a_f, a_i, a_g, b_q, b_k, b_v, c_z, c_xbc, c_dt, d_x, d_g) = jnp.split(proj, split_points, axis=-1)
        y_a = hgrn2_mixer(a_q, a_f, a_i, a_g, lower_bounds[l], hgrn_norm[l])
        y_b = dilated_attention(b_q, b_k, b_v, attn_norm[l])
        y_c = mamba2_mixer(c_z, c_xbc, c_dt, ssm_conv_w[l], ssm_conv_b[l], ssm_dt_bias[l],
                           ssm_a_log[l], ssm_d[l], ssm_norm[l])
        y_d = rglru_mixer(d_x, d_g, lru_conv_w[l], lru_conv_b[l], lru_w_a[l], lru_b_a[l],
                          lru_w_x[l], lru_b_x[l], lru_a_param[l], lru_norm[l])
        y = jnp.concatenate([y_a, y_b, y_c, y_d], axis=-1).astype(x.dtype)
        x = x + y @ w_out[l]
        x = x + 0.5 * swiglu(rmsnorm(x, ffn2_norm[l]), ffn2_w_gate[l], ffn2_w_up[l], ffn2_w_down[l])
    return rmsnorm(x, final_norm)
```

```python
import functools

import numpy as np
import jax
import jax.numpy as jnp
from jax import lax
from jax.experimental import pallas as pl
from jax.experimental.pallas import tpu as pltpu

F32 = jnp.float32
BF16 = jnp.bfloat16

NORM_EPS = 1e-6
MASK_VALUE = -1e30
GROUP_W = 512
HG_HEADS = 4
HG_HEAD_W = GROUP_W // HG_HEADS
ATT_HEAD_DIM = 64
ATT_SPAN = 128
ATT_DILATIONS = (1, 4, 16)
SSM_HEADS = 8
SSM_HEAD_DIM = 64
SSM_GROUPS = 2
SSM_STATE = 128
SSM_CONV_DIM = GROUP_W + 2 * SSM_GROUPS * SSM_STATE
CONV_W = 4
LRU_C = 8.0

LANES = 128
SUBLANES = 8
VMEM_LIMIT = 52 * 1024 * 1024

PROJ_BLOCKS = 13
PROJ_W = PROJ_BLOCKS * GROUP_W
BLK_A = 0
BLK_B = 4
BLK_CZ = 7
BLK_CXBC = 8
BLK_CDT = 10
BLK_DX = 11
BLK_DG = 12
N_IN_BEFORE_PAD = 10 * GROUP_W + SSM_HEADS

CHUNK = 256
CHUNK_LEVELS = 8


def _cparams(*sem):
    return pltpu.CompilerParams(dimension_semantics=sem, vmem_limit_bytes=VMEM_LIMIT)


def _rms(x, w):
    return x * lax.rsqrt(jnp.mean(x * x, axis=-1, keepdims=True) + NORM_EPS) * w


def _silu(x):
    return x * jax.nn.sigmoid(x)


def _softplus(x):
    return jnp.maximum(x, 0.0) + jnp.log1p(jnp.exp(-jnp.abs(x)))


def _expm1(u):
    w = jnp.exp(u)
    near = (jnp.abs(u) < 0.5) & (w != 1.0)
    kahan = (w - 1.0) * u / jnp.log(jnp.where(near, w, 2.0))
    return jnp.where(near, kahan, jnp.where(w == 1.0, u, w - 1.0))


def _dot(a, b):
    return jnp.dot(a, b, preferred_element_type=F32)


def _dot_nt(a, b):
    return lax.dot_general(a, b, (((1,), (1,)), ((), ())), preferred_element_type=F32)


def _dot_tn(a, b):
    return lax.dot_general(a, b, (((0,), (0,)), ((), ())), preferred_element_type=F32)


def _sum01_matmul(m01, x):
    hi = x.astype(BF16)
    r1 = x - hi.astype(F32)
    mid = r1.astype(BF16)
    lo = (r1 - mid.astype(F32)).astype(BF16)
    return _dot(m01, hi) + _dot(m01, mid) + _dot(m01, lo)


def _ffn_kernel(x_ref, nw_ref, wg_ref, wu_ref, wd_ref, fw_ref, o_ref, h_ref, *, n_f, final_norm):
    j = pl.program_id(1)

    @pl.when(j == 0)
    def _():
        h_ref[...] = _rms(x_ref[...], nw_ref[...]).astype(BF16)

    h = h_ref[...]
    a = (_silu(_dot(h, wg_ref[...])) * _dot(h, wu_ref[...])).astype(BF16)
    part = _dot(a, wd_ref[...])

    @pl.when(j == 0)
    def _():
        o_ref[...] = part

    @pl.when(j > 0)
    def _():
        o_ref[...] += part

    @pl.when(j == n_f - 1)
    def _():
        y = x_ref[...] + 0.5 * o_ref[...]
        if final_norm:
            y = _rms(y, fw_ref[...])
        o_ref[...] = y


def _ffn(x2, nw, wg, wu, wd, fw, *, final_norm, tm=512, tf=512):
    t, d = x2.shape
    f = wg.shape[1]
    n_f = f // tf
    return pl.pallas_call(
        functools.partial(_ffn_kernel, n_f=n_f, final_norm=final_norm),
        grid=(t // tm, n_f),
        in_specs=[
            pl.BlockSpec((tm, d), lambda i, j: (i, 0)),
            pl.BlockSpec((1, d), lambda i, j: (0, 0)),
            pl.BlockSpec((d, tf), lambda i, j: (0, j)),
            pl.BlockSpec((d, tf), lambda i, j: (0, j)),
            pl.BlockSpec((tf, d), lambda i, j: (j, 0)),
            pl.BlockSpec((1, d), lambda i, j: (0, 0)),
        ],
        out_specs=pl.BlockSpec((tm, d), lambda i, j: (i, 0)),
        out_shape=jax.ShapeDtypeStruct((t, d), F32),
        scratch_shapes=[pltpu.VMEM((tm, d), BF16)],
        compiler_params=_cparams("parallel", "arbitrary"),
        name="ffn",
    )(x2, nw, wg, wu, wd, fw)


def _inproj_kernel(x_ref, nw_ref, w_ref, o_ref, h_ref):
    @pl.when(pl.program_id(1) == 0)
    def _():
        h_ref[...] = _rms(x_ref[...], nw_ref[...]).astype(BF16)

    o_ref[...] = _dot(h_ref[...], w_ref[...])


def _inproj(x2, nw, w, *, tm=1024, tn=512):
    t, d = x2.shape
    n = w.shape[1]
    return pl.pallas_call(
        _inproj_kernel,
        grid=(t // tm, n // tn),
        in_specs=[
            pl.BlockSpec((tm, d), lambda i, j: (i, 0)),
            pl.BlockSpec((1, d), lambda i, j: (0, 0)),
            pl.BlockSpec((d, tn), lambda i, j: (0, j)),
        ],
        out_specs=pl.BlockSpec((tm, tn), lambda i, j: (i, j)),
        out_shape=jax.ShapeDtypeStruct((t, n), F32),
        scratch_shapes=[pltpu.VMEM((tm, d), BF16)],
        compiler_params=_cparams("parallel", "arbitrary"),
        name="inproj",
    )(x2, nw, w)


def _outproj_kernel(x_ref, ya_ref, yb_ref, yc_ref, yd_ref, w_ref, o_ref):
    acc = x_ref[...]
    for g, y_ref in enumerate((ya_ref, yb_ref, yc_ref, yd_ref)):
        acc = acc + _dot(y_ref[...], w_ref[g * GROUP_W:(g + 1) * GROUP_W, :])
    o_ref[...] = acc


def _outproj(x2, ya, yb, yc, yd, w, *, tm=512):
    t, d = x2.shape
    yspec = pl.BlockSpec((tm, GROUP_W), lambda i: (i, 0))
    return pl.pallas_call(
        _outproj_kernel,
        grid=(t // tm,),
        in_specs=[pl.BlockSpec((tm, d), lambda i: (i, 0)), yspec, yspec, yspec, yspec,
                  pl.BlockSpec((4 * GROUP_W, d), lambda i: (0, 0))],
        out_specs=pl.BlockSpec((tm, d), lambda i: (i, 0)),
        out_shape=jax.ShapeDtypeStruct((t, d), F32),
        compiler_params=_cparams("parallel"),
        name="outproj",
    )(x2, ya, yb, yc, yd, w)


def _hgrn_tables(c=CHUNK, levels=CHUNK_LEVELS):
    t = np.arange(c)
    rows = []
    for lvl in range(levels):
        h = 1 << lvl
        r = (t // (2 * h)) * (2 * h) + h
        up = (t // h) % 2 == 1
        u = t[None, :]
        m = np.where(up[:, None], (u > r[:, None]) & (u <= t[:, None]), (u > t[:, None]) & (u <= r[:, None]))
        rows.append(m)
    rows.append(t[None, :] <= t[:, None])
    rows.append(t[None, :] > t[:, None])
    mstack = np.concatenate(rows, axis=0).astype(np.float32)
    x = t[:, None] ^ t[None, :]
    lv = np.where(t[None, :] < t[:, None], np.floor(np.log2(np.maximum(x, 1))).astype(np.int32), -1)
    return mstack, lv.astype(np.int32)


_HGRN_MSTACK, _HGRN_LEVEL = _hgrn_tables()


def _hgrn_kernel(q_ref, f_ref, i_ref, g_ref, lbl_ref, nw_ref, mst_ref, lv_ref, o_ref, st_ref, *, layer):
    c = CHUNK

    @pl.when(pl.program_id(1) == 0)
    def _():
        st_ref[...] = jnp.zeros_like(st_ref)

    lbl = lbl_ref[...]
    e = jnp.exp(lbl - jnp.max(lbl, axis=0, keepdims=True))
    p = e / jnp.sum(e, axis=0, keepdims=True)
    lb = jnp.zeros((1, GROUP_W), F32)
    for i in range(1, layer + 1):
        lb = lb + p[i:i + 1, :]

    mst = mst_ref[...]
    lv = lv_ref[...]
    row = lax.broadcasted_iota(jnp.int32, (c, 1), 0)
    for h in range(HG_HEADS):
        sl = slice(h * HG_HEAD_W, (h + 1) * HG_HEAD_W)
        q = _silu(q_ref[:, sl])
        z = f_ref[:, sl]
        v = i_ref[:, sl]
        lbh = lb[:, sl]
        g = jnp.log(lbh + (1.0 - lbh) * jax.nn.sigmoid(z))
        k = (1.0 - lbh) * jax.nn.sigmoid(-z)
        ex = jnp.exp(_sum01_matmul(mst, g))
        a = jnp.zeros((c, c), F32)
        for lvl in range(CHUNK_LEVELS):
            exl = ex[lvl * c:(lvl + 1) * c]
            up = ((row >> lvl) & 1) == 1
            qp = jnp.where(up, q * exl, 0.0).astype(BF16)
            kp = jnp.where(up, 0.0, k * exl).astype(BF16)
            a = jnp.where(lv == lvl, _dot_nt(qp, kp), a)
        vb = v.astype(BF16)
        ex_b = ex[CHUNK_LEVELS * c:(CHUNK_LEVELS + 1) * c]
        ex_r = ex[(CHUNK_LEVELS + 1) * c:(CHUNK_LEVELS + 2) * c]
        st = st_ref[h]
        o = (_dot(a.astype(BF16), vb) + jnp.sum(q * k, axis=-1, keepdims=True) * v
             + _dot_nt((q * ex_b).astype(BF16), st.astype(BF16)))
        st_ref[h] = ex_b[c - 1:c, :] * st + _dot_tn(vb, (k * ex_r).astype(BF16))
        o = _rms(o, nw_ref[:, sl]) * _silu(g_ref[:, sl])
        o_ref[:, sl] = o.astype(o_ref.dtype)


def _hgrn(proj3, lb_logits, norm_w, *, layer):
    b, s, _ = proj3.shape
    cspec = lambda blk: pl.BlockSpec((None, CHUNK, GROUP_W), lambda bi, ci: (bi, ci, blk))
    full = lambda shape: pl.BlockSpec(shape, lambda bi, ci: (0,) * len(shape))
    mst = jnp.asarray(_HGRN_MSTACK, BF16)
    lv = jnp.asarray(_HGRN_LEVEL)
    return pl.pallas_call(
        functools.partial(_hgrn_kernel, layer=layer),
        grid=(b, s // CHUNK),
        in_specs=[cspec(BLK_A), cspec(BLK_A + 1), cspec(BLK_A + 2), cspec(BLK_A + 3),
                  full(lb_logits.shape), full((1, GROUP_W)), full(mst.shape), full(lv.shape)],
        out_specs=pl.BlockSpec((None, CHUNK, GROUP_W), lambda bi, ci: (bi, ci, 0)),
        out_shape=jax.ShapeDtypeStruct((b, s, GROUP_W), BF16),
        scratch_shapes=[pltpu.VMEM((HG_HEADS, HG_HEAD_W, HG_HEAD_W), F32)],
        compiler_params=_cparams("parallel", "arbitrary"),
        name="hgrn2",
    )(proj3, proj3, proj3, proj3, lb_logits, norm_w, mst, lv)


def _attn_kernel(q_ref, kc_ref, vc_ref, kp_ref, vp_ref, o_ref, l_ref, kcat_ref, vcat_ref, *, qb):
    n = pl.program_id(2)
    blk = ATT_SPAN
    kcat_ref[0:blk, :] = kp_ref[...].astype(BF16)
    kcat_ref[blk:blk + qb, :] = kc_ref[...].astype(BF16)
    vcat_ref[0:blk, :] = vp_ref[...].astype(BF16)
    vcat_ref[blk:blk + qb, :] = vc_ref[...].astype(BF16)

    iq = lax.broadcasted_iota(jnp.int32, (blk, blk), 0)
    ik = lax.broadcasted_iota(jnp.int32, (blk, blk), 1)
    lane = lax.broadcasted_iota(jnp.int32, (1, LANES), 1)
    first_head = lane < ATT_HEAD_DIM
    scale = ATT_HEAD_DIM ** -0.5

    def sub_block(sb, carry):
        r0 = pl.multiple_of(sb * blk, blk)
        no_prev = 1 - jnp.minimum(n * (qb // blk) + sb, 1)
        valid_p = ik >= iq + no_prev * blk
        valid_c = ik <= iq
        for pair in range(GROUP_W // LANES):
            ls = slice(pair * LANES, (pair + 1) * LANES)
            q2 = q_ref[pl.ds(r0, blk), ls] * scale
            k_p = kcat_ref[pl.ds(r0, blk), ls]
            k_c = kcat_ref[pl.ds(r0 + blk, blk), ls]
            v_p = vcat_ref[pl.ds(r0, blk), ls]
            v_c = vcat_ref[pl.ds(r0 + blk, blk), ls]
            outs, lses = [], []
            for hh in range(2):
                sel = first_head if hh == 0 else jnp.logical_not(first_head)
                qm = jnp.where(sel, q2, 0.0).astype(BF16)
                s_p = jnp.where(valid_p, _dot_nt(qm, k_p), MASK_VALUE)
                s_c = jnp.where(valid_c, _dot_nt(qm, k_c), MASK_VALUE)
                m = jnp.maximum(jnp.max(s_p, axis=-1, keepdims=True), jnp.max(s_c, axis=-1, keepdims=True))
                p_p = jnp.where(valid_p, jnp.exp(s_p - m), 0.0)
                p_c = jnp.where(valid_c, jnp.exp(s_c - m), 0.0)
                den = jnp.sum(p_p, axis=-1, keepdims=True) + jnp.sum(p_c, axis=-1, keepdims=True)
                num = _dot(p_p.astype(BF16), v_p) + _dot(p_c.astype(BF16), v_c)
                outs.append(num / den)
                lses.append(jnp.broadcast_to(m + jnp.log(den), (blk, LANES)))
            o_ref[pl.ds(r0, blk), ls] = jnp.where(first_head, outs[0], outs[1])
            l_ref[pl.ds(r0, blk), ls] = jnp.where(first_head, lses[0], lses[1])
        return carry

    lax.fori_loop(0, qb // blk, sub_block, 0)


def _attn_pattern(proj3, dil):
    b, s, w = proj3.shape
    m_len = s // dil
    qb = min(512, m_len)
    view = proj3.reshape(b, m_len, dil * w)
    nblk = PROJ_BLOCKS
    per = qb // ATT_SPAN
    cur = lambda off: pl.BlockSpec((None, qb, GROUP_W), lambda bi, r, n: (bi, n, r * nblk + BLK_B + off))
    prev = lambda off: pl.BlockSpec((None, ATT_SPAN, GROUP_W),
                                    lambda bi, r, n: (bi, jnp.maximum(n * per - 1, 0), r * nblk + BLK_B + off))
    ospec = pl.BlockSpec((None, qb, GROUP_W), lambda bi, r, n: (bi, n, r))
    oshape = jax.ShapeDtypeStruct((b, m_len, dil * GROUP_W), F32)
    o, l = pl.pallas_call(
        functools.partial(_attn_kernel, qb=qb),
        grid=(b, dil, m_len // qb),
        in_specs=[cur(0), cur(1), cur(2), prev(1), prev(2)],
        out_specs=[ospec, ospec],
        out_shape=[oshape, oshape],
        scratch_shapes=[pltpu.VMEM((ATT_SPAN + qb, GROUP_W), BF16), pltpu.VMEM((ATT_SPAN + qb, GROUP_W), BF16)],
        compiler_params=_cparams("parallel", "parallel", "arbitrary"),
        name=f"dilattn_d{dil}",
    )(view, view, view, view, view)
    return o.reshape(b * s, GROUP_W), l.reshape(b * s, GROUP_W)


def _attn_merge_kernel(o1, l1, o2, l2, o3, l3, nw_ref, y_ref):
    la, lb, lc = l1[...], l2[...], l3[...]
    m = jnp.maximum(jnp.maximum(la, lb), lc)
    wa, wb, wc = jnp.exp(la - m), jnp.exp(lb - m), jnp.exp(lc - m)
    o = (wa * o1[...] + wb * o2[...] + wc * o3[...]) / (wa + wb + wc)
    y_ref[...] = _rms(o, nw_ref[...]).astype(y_ref.dtype)


def _attn_merge(parts, norm_w, *, tm=1024):
    t = parts[0].shape[0]
    spec = pl.BlockSpec((tm, GROUP_W), lambda i: (i, 0))
    return pl.pallas_call(
        _attn_merge_kernel,
        grid=(t // tm,),
        in_specs=[spec] * 6 + [pl.BlockSpec((1, GROUP_W), lambda i: (0, 0))],
        out_specs=spec,
        out_shape=jax.ShapeDtypeStruct((t, GROUP_W), BF16),
        compiler_params=_cparams("parallel"),
        name="dilattn_merge",
    )(*parts, norm_w)


def _causal_conv(x_ref, xpad_ref, w_ref, b_ref, first):
    c = x_ref.shape[0]

    @pl.when(first)
    def _():
        xpad_ref[0:SUBLANES, :] = jnp.zeros((SUBLANES, xpad_ref.shape[1]), F32)

    xpad_ref[SUBLANES:SUBLANES + c, :] = x_ref[...]
    acc = b_ref[...] + w_ref[0:1, :] * xpad_ref[pl.ds(SUBLANES - CONV_W + 1, c), :]
    for j in range(1, CONV_W):
        acc = acc + w_ref[j:j + 1, :] * xpad_ref[pl.ds(SUBLANES - CONV_W + 1 + j, c), :]
    xpad_ref[0:SUBLANES, :] = xpad_ref[c:c + SUBLANES, :]
    return acc


def _pair_lanes(col_fn, first_head):
    return jnp.where(first_head, col_fn(0), col_fn(1))


def _ssd_kernel(z_ref, xbc_ref, dt_ref, cw_ref, cb_ref, dtb_ref, alog_ref, dsk_ref, nw_ref, tril_ref,
                y_ref, xpad_ref, st_ref):
    c = CHUNK
    first = pl.program_id(1) == 0

    @pl.when(first)
    def _():
        st_ref[...] = jnp.zeros_like(st_ref)

    xbc = _silu(_causal_conv(xbc_ref, xpad_ref, cw_ref, cb_ref, first))
    dt = _softplus(dt_ref[...] + dtb_ref[...])
    adt = dt * (-jnp.exp(alog_ref[...]))
    tril01 = tril_ref[...]
    acs = _sum01_matmul(tril01, adt)
    acs_t = acs.T
    e_cs = jnp.exp(acs)
    dt_rest = dt * jnp.exp(acs[c - 1:c, :] - acs)
    causal = tril01 > 0

    lane = lax.broadcasted_iota(jnp.int32, (1, LANES), 1)
    first_head = lane < SSM_HEAD_DIM
    gw = GROUP_W // SSM_GROUPS
    hpg = SSM_HEADS // SSM_GROUPS
    for g in range(SSM_GROUPS):
        bg = xbc[:, GROUP_W + g * SSM_STATE:GROUP_W + (g + 1) * SSM_STATE]
        cg = xbc[:, GROUP_W + (SSM_GROUPS + g) * SSM_STATE:GROUP_W + (SSM_GROUPS + g + 1) * SSM_STATE].astype(BF16)
        cb = _dot_nt(cg, bg.astype(BF16))
        st = st_ref[g]
        y_off = _dot(cg, st.astype(BF16))
        ys, xds, dec_last = [], [], []
        for pr in range(hpg // 2):
            h0 = g * hpg + 2 * pr
            ls = slice(h0 * SSM_HEAD_DIM, (h0 + 2) * SSM_HEAD_DIM)
            xs = xbc[:, ls]
            xdt = (xs * _pair_lanes(lambda i: dt[:, h0 + i:h0 + i + 1], first_head)).astype(BF16)
            yd = []
            for i in range(2):
                hh = h0 + i
                seg = acs[:, hh:hh + 1] - acs_t[hh:hh + 1, :]
                lmat = jnp.where(causal, jnp.exp(jnp.where(causal, seg, 0.0)), 0.0)
                yd.append(_dot((cb * lmat).astype(BF16), xdt))
            y_pair = (jnp.where(first_head, yd[0], yd[1])
                      + y_off[:, pr * LANES:(pr + 1) * LANES] * _pair_lanes(lambda i: e_cs[:, h0 + i:h0 + i + 1], first_head)
                      + dsk_ref[:, ls] * xs)
            ys.append(y_pair)
            xds.append((xs * _pair_lanes(lambda i: dt_rest[:, h0 + i:h0 + i + 1], first_head)).astype(BF16))
            dec_last.append(_pair_lanes(lambda i: e_cs[c - 1:c, h0 + i:h0 + i + 1], first_head))
        st_ref[g] = (jnp.concatenate(dec_last, axis=1) * st
                     + _dot_tn(bg.astype(BF16), jnp.concatenate(xds, axis=1)))
        yg = jnp.concatenate(ys, axis=1) * _silu(z_ref[:, g * gw:(g + 1) * gw])
        y_ref[:, g * gw:(g + 1) * gw] = _rms(yg, nw_ref[:, g * gw:(g + 1) * gw]).astype(y_ref.dtype)


def _ssd(proj3, conv_w, conv_b, dt_bias, a_log, d_skip, norm_w):
    b, s, _ = proj3.shape
    full = lambda shape: pl.BlockSpec(shape, lambda bi, ci: (0,) * len(shape))
    tril = jnp.asarray(np.tril(np.ones((CHUNK, CHUNK), np.float32)), BF16)
    return pl.pallas_call(
        _ssd_kernel,
        grid=(b, s // CHUNK),
        in_specs=[
            pl.BlockSpec((None, CHUNK, GROUP_W), lambda bi, ci: (bi, ci, BLK_CZ)),
            pl.BlockSpec((None, CHUNK, SSM_CONV_DIM), lambda bi, ci: (bi, ci, BLK_CXBC * GROUP_W // SSM_CONV_DIM)),
            pl.BlockSpec((None, CHUNK, LANES), lambda bi, ci: (bi, ci, BLK_CDT * GROUP_W // LANES)),
            full((CONV_W, SSM_CONV_DIM)), full((1, SSM_CONV_DIM)), full((1, LANES)), full((1, LANES)),
            full((1, GROUP_W)), full((1, GROUP_W)), full((CHUNK, CHUNK)),
        ],
        out_specs=pl.BlockSpec((None, CHUNK, GROUP_W), lambda bi, ci: (bi, ci, 0)),
        out_shape=jax.ShapeDtypeStruct((b, s, GROUP_W), BF16),
        scratch_shapes=[pltpu.VMEM((CHUNK + SUBLANES, SSM_CONV_DIM), F32),
                        pltpu.VMEM((SSM_GROUPS, SSM_STATE, GROUP_W // SSM_GROUPS), F32)],
        compiler_params=_cparams("parallel", "arbitrary"),
        name="ssd",
    )(proj3, proj3, proj3, conv_w, conv_b, dt_bias, a_log, d_skip, norm_w, tril)


def _gelu_tanh(x):
    return 0.5 * x * (1.0 + jnp.tanh(np.sqrt(2.0 / np.pi).astype(np.float32) * (x + 0.044715 * (x * x * x))))


def _lru_kernel(x_ref, g_ref, cw_ref, cb_ref, wa_ref, ba_ref, wx_ref, bx_ref, ap_ref, nw_ref,
                y_ref, xpad_ref, h_ref):
    c = CHUNK
    first = pl.program_id(1) == 0

    @pl.when(first)
    def _():
        h_ref[...] = jnp.zeros_like(h_ref)

    xc = _causal_conv(x_ref, xpad_ref, cw_ref, cb_ref, first)
    xb = xc.astype(BF16)
    r = jax.nn.sigmoid(_dot(xb, wa_ref[...]) + ba_ref[...])
    i = jax.nn.sigmoid(_dot(xb, wx_ref[...]) + bx_ref[...])
    log_a = -LRU_C * r * _softplus(-ap_ref[...])
    a = jnp.exp(log_a)
    bt = jnp.sqrt(jnp.maximum(-_expm1(2.0 * log_a), 0.0)) * (i * xc)
    row = lax.broadcasted_iota(jnp.int32, (c, 1), 0)
    d = 1
    while d < c:
        keep = row >= d
        a_s = pltpu.roll(a, d, 0)
        b_s = pltpu.roll(bt, d, 0)
        bt = jnp.where(keep, a * b_s + bt, bt)
        a = jnp.where(keep, a * a_s, a)
        d *= 2
    h = bt + a * h_ref[...]
    h_ref[...] = h[c - 1:c, :]
    y_ref[...] = _rms(h * _gelu_tanh(g_ref[...]), nw_ref[...]).astype(y_ref.dtype)


def _lru(proj3, conv_w, conv_b, wa_bd, b_a, wx_bd, b_x, a_param, norm_w):
    b, s, _ = proj3.shape
    full = lambda shape: pl.BlockSpec(shape, lambda bi, ci: (0,) * len(shape))
    vec = full((1, GROUP_W))
    return pl.pallas_call(
        _lru_kernel,
        grid=(b, s // CHUNK),
        in_specs=[
            pl.BlockSpec((None, CHUNK, GROUP_W), lambda bi, ci: (bi, ci, BLK_DX)),
            pl.BlockSpec((None, CHUNK, GROUP_W), lambda bi, ci: (bi, ci, BLK_DG)),
            full((CONV_W, GROUP_W)), vec, full((GROUP_W, GROUP_W)), vec, full((GROUP_W, GROUP_W)), vec, vec, vec,
        ],
        out_specs=pl.BlockSpec((None, CHUNK, GROUP_W), lambda bi, ci: (bi, ci, 0)),
        out_shape=jax.ShapeDtypeStruct((b, s, GROUP_W), BF16),
        scratch_shapes=[pltpu.VMEM((CHUNK + SUBLANES, GROUP_W), F32), pltpu.VMEM((1, GROUP_W), F32)],
        compiler_params=_cparams("parallel", "arbitrary"),
        name="rglru",
    )(proj3, proj3, conv_w, conv_b, wa_bd, b_a, wx_bd, b_x, a_param, norm_w)


def _block_diag(w):
    nb, n, _ = w.shape
    eye = jnp.eye(nb, dtype=w.dtype)
    return jnp.einsum("hij,hg->higj", w, eye).reshape(nb * n, nb * n)


def _pad_lanes(v, width=LANES):
    return jnp.pad(v, (0, width - v.shape[0])).reshape(1, width)


def kernel(x, ffn1_norm, ffn1_w_gate, ffn1_w_up, ffn1_w_down, mix_norm, w_in, w_out, hgrn_lb_logits, hgrn_norm, attn_norm, ssm_conv_w, ssm_conv_b, ssm_dt_bias, ssm_a_log, ssm_d, ssm_norm, lru_conv_w, lru_conv_b, lru_w_a, lru_b_a, lru_w_x, lru_b_x, lru_a_param, lru_norm, ffn2_norm, ffn2_w_gate, ffn2_w_up, ffn2_w_down, final_norm):
    bsz, seq, d = x.shape
    depth = w_in.shape[0]
    t = bsz * seq
    row = lambda v: v.reshape(1, -1).astype(F32)
    x2 = x.reshape(t, d)
    lb_logits = hgrn_lb_logits.astype(F32)
    fnorm = row(final_norm)
    for l in range(depth):
        x2 = _ffn(x2, row(ffn1_norm[l]), ffn1_w_gate[l].astype(BF16), ffn1_w_up[l].astype(BF16),
                  ffn1_w_down[l].astype(BF16), fnorm, final_norm=False)
        w_in_l = w_in[l]
        w_pad = jnp.concatenate(
            [w_in_l[:, :N_IN_BEFORE_PAD],
             jnp.zeros((d, (BLK_DX * GROUP_W) - N_IN_BEFORE_PAD), w_in_l.dtype),
             w_in_l[:, N_IN_BEFORE_PAD:]], axis=1).astype(BF16)
        proj3 = _inproj(x2, row(mix_norm[l]), w_pad).reshape(bsz, seq, PROJ_W)
        y_a = _hgrn(proj3, lb_logits, row(hgrn_norm[l]), layer=l).reshape(t, GROUP_W)
        parts = []
        for dil in ATT_DILATIONS:
            parts.extend(_attn_pattern(proj3, dil))
        y_b = _attn_merge(parts, row(attn_norm[l]))
        y_c = _ssd(proj3, ssm_conv_w[l].astype(F32), row(ssm_conv_b[l]), _pad_lanes(ssm_dt_bias[l].astype(F32)),
                   _pad_lanes(ssm_a_log[l].astype(F32)), row(jnp.repeat(ssm_d[l], SSM_HEAD_DIM)),
                   row(ssm_norm[l])).reshape(t, GROUP_W)
        y_d = _lru(proj3, lru_conv_w[l].astype(F32), row(lru_conv_b[l]), _block_diag(lru_w_a[l]).astype(BF16),
                   row(lru_b_a[l]), _block_diag(lru_w_x[l]).astype(BF16), row(lru_b_x[l]), row(lru_a_param[l]),
                   row(lru_norm[l])).reshape(t, GROUP_W)
        x2 = _outproj(x2, y_a, y_b, y_c, y_d, w_out[l].astype(BF16))
        x2 = _ffn(x2, row(ffn2_norm[l]), ffn2_w_gate[l].astype(BF16), ffn2_w_up[l].astype(BF16),
                  ffn2_w_down[l].astype(BF16), fnorm, final_norm=(l == depth - 1))
    return x2.reshape(bsz, seq, d)
```

```python
import functools

import numpy as np
import jax
import jax.numpy as jnp
from jax import lax
from jax.experimental import pallas as pl
from jax.experimental.pallas import tpu as pltpu

F32 = jnp.float32
BF16 = jnp.bfloat16

NORM_EPS = 1e-6
MASK_VALUE = -1e30
GROUP_W = 512
HG_HEADS = 4
HG_HEAD_W = GROUP_W // HG_HEADS
ATT_HEAD_DIM = 64
ATT_SPAN = 128
ATT_DILATIONS = (1, 4, 16)
SSM_HEADS = 8
SSM_HEAD_DIM = 64
SSM_GROUPS = 2
SSM_STATE = 128
SSM_CONV_DIM = GROUP_W + 2 * SSM_GROUPS * SSM_STATE
CONV_W = 4
LRU_C = 8.0

LANES = 128
SUBLANES = 8
VMEM_LIMIT = 52 * 1024 * 1024

PROJ_BLOCKS = 13
PROJ_W = PROJ_BLOCKS * GROUP_W
BLK_A = 0
BLK_B = 4
BLK_CZ = 7
BLK_CXBC = 8
BLK_CDT = 10
BLK_DX = 11
BLK_DG = 12
N_IN_BEFORE_PAD = 10 * GROUP_W + SSM_HEADS

CHUNK = 256
CHUNK_LEVELS = 8


def _cparams(*sem):
    return pltpu.CompilerParams(dimension_semantics=sem, vmem_limit_bytes=VMEM_LIMIT)


def _rms(x, w):
    return x * lax.rsqrt(jnp.mean(x * x, axis=-1, keepdims=True) + NORM_EPS) * w


def _silu(x):
    return x * jax.nn.sigmoid(x)


def _softplus(x):
    return jnp.maximum(x, 0.0) + jnp.log1p(jnp.exp(-jnp.abs(x)))


def _expm1(u):
    w = jnp.exp(u)
    near = (jnp.abs(u) < 0.5) & (w != 1.0)
    kahan = (w - 1.0) * u / jnp.log(jnp.where(near, w, 2.0))
    return jnp.where(near, kahan, jnp.where(w == 1.0, u, w - 1.0))


def _dot(a, b):
    return jnp.dot(a, b, preferred_element_type=F32)


def _dot_nt(a, b):
    return lax.dot_general(a, b, (((1,), (1,)), ((), ())), preferred_element_type=F32)


def _dot_tn(a, b):
    return lax.dot_general(a, b, (((0,), (0,)), ((), ())), preferred_element_type=F32)


def _sum01_matmul(m01, x):
    hi = x.astype(BF16)
    r1 = x - hi.astype(F32)
    mid = r1.astype(BF16)
    lo = (r1 - mid.astype(F32)).astype(BF16)
    return _dot(m01, hi) + _dot(m01, mid) + _dot(m01, lo)


def _sum01_matmul2(m01, x):
    hi = x.astype(BF16)
    mid = (x - hi.astype(F32)).astype(BF16)
    both = _dot(m01, jnp.concatenate([hi, mid], axis=1))
    n = x.shape[1]
    return both[:, :n] + both[:, n:]


def _ffn_kernel(x_ref, xc_ref, nw_ref, wg_ref, wu_ref, wd_ref, o_ref, h_ref, a_ref, *, n_f, tf):
    j = pl.program_id(1)

    @pl.when(j == 0)
    def _():
        h_ref[...] = _rms(x_ref[...], nw_ref[...]).astype(BF16)

    @pl.when(j < n_f)
    def _():
        h = h_ref[...]
        a_ref[j] = (_silu(_dot(h, wg_ref[...])) * _dot(h, wu_ref[...])).astype(BF16)

    @pl.when(j >= n_f)
    def _():
        acc = _dot(a_ref[0], wd_ref[0:tf, :])
        for jj in range(1, n_f):
            acc = acc + _dot(a_ref[jj], wd_ref[jj * tf:(jj + 1) * tf, :])
        o_ref[...] = xc_ref[...] + 0.5 * acc


def _ffn(x2, nw, wg, wu, wd, *, tm=512, tf=512, tn=512):
    t, d = x2.shape
    f = wg.shape[1]
    n_f, n_d = f // tf, d // tn
    up = lambda i, j: (0, jnp.minimum(j, n_f - 1))
    down = lambda i, j: (0, jnp.maximum(j - n_f, 0))
    col = lambda i, j: (i, jnp.maximum(j - n_f, 0))
    return pl.pallas_call(
        functools.partial(_ffn_kernel, n_f=n_f, tf=tf),
        grid=(t // tm, n_f + n_d),
        in_specs=[
            pl.BlockSpec((tm, d), lambda i, j: (i, 0)),
            pl.BlockSpec((tm, tn), col),
            pl.BlockSpec((1, d), lambda i, j: (0, 0)),
            pl.BlockSpec((d, tf), up),
            pl.BlockSpec((d, tf), up),
            pl.BlockSpec((f, tn), down),
        ],
        out_specs=pl.BlockSpec((tm, tn), col),
        out_shape=jax.ShapeDtypeStruct((t, d), F32),
        scratch_shapes=[pltpu.VMEM((tm, d), BF16), pltpu.VMEM((n_f, tm, tf), BF16)],
        compiler_params=_cparams("parallel", "arbitrary"),
        name="ffn",
    )(x2, x2, nw, wg, wu, wd)


def _final_norm_kernel(x_ref, w_ref, o_ref):
    o_ref[...] = _rms(x_ref[...], w_ref[...])


def _final_norm(x2, w, *, tm=1024):
    t, d = x2.shape
    return pl.pallas_call(
        _final_norm_kernel,
        grid=(t // tm,),
        in_specs=[pl.BlockSpec((tm, d), lambda i: (i, 0)), pl.BlockSpec((1, d), lambda i: (0, 0))],
        out_specs=pl.BlockSpec((tm, d), lambda i: (i, 0)),
        out_shape=jax.ShapeDtypeStruct((t, d), F32),
        compiler_params=_cparams("parallel"),
        name="final_norm",
    )(x2, w)


def _inproj_kernel(x_ref, nw_ref, w_ref, o_ref, h_ref):
    @pl.when(pl.program_id(1) == 0)
    def _():
        h_ref[...] = _rms(x_ref[...], nw_ref[...]).astype(BF16)

    o_ref[...] = _dot(h_ref[...], w_ref[...])


def _inproj(x2, nw, w, *, tm=1024, tn=512):
    t, d = x2.shape
    n = w.shape[1]
    return pl.pallas_call(
        _inproj_kernel,
        grid=(t // tm, n // tn),
        in_specs=[
            pl.BlockSpec((tm, d), lambda i, j: (i, 0)),
            pl.BlockSpec((1, d), lambda i, j: (0, 0)),
            pl.BlockSpec((d, tn), lambda i, j: (0, j)),
        ],
        out_specs=pl.BlockSpec((tm, tn), lambda i, j: (i, j)),
        out_shape=jax.ShapeDtypeStruct((t, n), F32),
        scratch_shapes=[pltpu.VMEM((tm, d), BF16)],
        compiler_params=_cparams("parallel", "arbitrary"),
        name="inproj",
    )(x2, nw, w)


def _outproj_kernel(x_ref, ya_ref, yb_ref, yc_ref, yd_ref, bnw_ref, w_ref, o_ref):
    yb = _rms(yb_ref[...], bnw_ref[...]).astype(BF16)
    acc = x_ref[...]
    for g, y in enumerate((ya_ref[...], yb, yc_ref[...], yd_ref[...])):
        acc = acc + _dot(y, w_ref[g * GROUP_W:(g + 1) * GROUP_W, :])
    o_ref[...] = acc


def _outproj(x2, ya, yb, yc, yd, attn_norm_w, w, *, tm=512):
    t, d = x2.shape
    yspec = pl.BlockSpec((tm, GROUP_W), lambda i: (i, 0))
    return pl.pallas_call(
        _outproj_kernel,
        grid=(t // tm,),
        in_specs=[pl.BlockSpec((tm, d), lambda i: (i, 0)), yspec, yspec, yspec, yspec,
                  pl.BlockSpec((1, GROUP_W), lambda i: (0, 0)), pl.BlockSpec((4 * GROUP_W, d), lambda i: (0, 0))],
        out_specs=pl.BlockSpec((tm, d), lambda i: (i, 0)),
        out_shape=jax.ShapeDtypeStruct((t, d), F32),
        compiler_params=_cparams("parallel"),
        name="outproj",
    )(x2, ya, yb, yc, yd, attn_norm_w, w)


def _hgrn_tables(c=CHUNK, levels=CHUNK_LEVELS):
    t = np.arange(c)
    rows = []
    for lvl in range(levels):
        h = 1 << lvl
        r = (t // (2 * h)) * (2 * h) + h
        up = (t // h) % 2 == 1
        u = t[None, :]
        m = np.where(up[:, None], (u > r[:, None]) & (u <= t[:, None]), (u > t[:, None]) & (u <= r[:, None]))
        rows.append(m)
    rows.append(t[None, :] <= t[:, None])
    rows.append(t[None, :] > t[:, None])
    mstack = np.concatenate(rows, axis=0).astype(np.float32)
    x = t[:, None] ^ t[None, :]
    lv = np.where(t[None, :] < t[:, None], np.floor(np.log2(np.maximum(x, 1))).astype(np.int32), -1)
    return mstack, lv.astype(np.int32)


_HGRN_MSTACK, _HGRN_LEVEL = _hgrn_tables()


def _hgrn_kernel(q_ref, f_ref, i_ref, g_ref, lbl_ref, nw_ref, mst_ref, lv_ref, o_ref, st_ref, *, layer):
    c = CHUNK

    @pl.when(pl.program_id(1) == 0)
    def _():
        st_ref[...] = jnp.zeros_like(st_ref)

    lbl = lbl_ref[...]
    e = jnp.exp(lbl - jnp.max(lbl, axis=0, keepdims=True))
    p = e / jnp.sum(e, axis=0, keepdims=True)
    lb = jnp.zeros((1, GROUP_W), F32)
    for i in range(1, layer + 1):
        lb = lb + p[i:i + 1, :]

    mst = mst_ref[...]
    lv = lv_ref[...]
    row = lax.broadcasted_iota(jnp.int32, (c, 1), 0)
    for h in range(HG_HEADS):
        sl = slice(h * HG_HEAD_W, (h + 1) * HG_HEAD_W)
        q = _silu(q_ref[:, sl])
        z = f_ref[:, sl]
        v = i_ref[:, sl]
        lbh = lb[:, sl]
        g = jnp.log(lbh + (1.0 - lbh) * jax.nn.sigmoid(z))
        k = (1.0 - lbh) * jax.nn.sigmoid(-z)
        ex = jnp.exp(_sum01_matmul2(mst, g))
        a = jnp.zeros((c, c), F32)
        for lvl in range(CHUNK_LEVELS):
            exl = ex[lvl * c:(lvl + 1) * c]
            up = ((row >> lvl) & 1) == 1
            qp = jnp.where(up, q * exl, 0.0).astype(BF16)
            kp = jnp.where(up, 0.0, k * exl).astype(BF16)
            a = jnp.where(lv == lvl, _dot_nt(qp, kp), a)
        vb = v.astype(BF16)
        ex_b = ex[CHUNK_LEVELS * c:(CHUNK_LEVELS + 1) * c]
        ex_r = ex[(CHUNK_LEVELS + 1) * c:(CHUNK_LEVELS + 2) * c]
        st = st_ref[h]
        o = (_dot(a.astype(BF16), vb) + jnp.sum(q * k, axis=-1, keepdims=True) * v
             + _dot_nt((q * ex_b).astype(BF16), st.astype(BF16)))
        st_ref[h] = ex_b[c - 1:c, :] * st + _dot_tn(vb, (k * ex_r).astype(BF16))
        o = _rms(o, nw_ref[:, sl]) * _silu(g_ref[:, sl])
        o_ref[:, sl] = o.astype(o_ref.dtype)


def _hgrn(proj3, lb_logits, norm_w, *, layer):
    b, s, _ = proj3.shape
    cspec = lambda blk: pl.BlockSpec((None, CHUNK, GROUP_W), lambda bi, ci: (bi, ci, blk))
    full = lambda shape: pl.BlockSpec(shape, lambda bi, ci: (0,) * len(shape))
    mst = jnp.asarray(_HGRN_MSTACK, BF16)
    lv = jnp.asarray(_HGRN_LEVEL)
    return pl.pallas_call(
        functools.partial(_hgrn_kernel, layer=layer),
        grid=(b, s // CHUNK),
        in_specs=[cspec(BLK_A), cspec(BLK_A + 1), cspec(BLK_A + 2), cspec(BLK_A + 3),
                  full(lb_logits.shape), full((1, GROUP_W)), full(mst.shape), full(lv.shape)],
        out_specs=pl.BlockSpec((None, CHUNK, GROUP_W), lambda bi, ci: (bi, ci, 0)),
        out_shape=jax.ShapeDtypeStruct((b, s, GROUP_W), BF16),
        scratch_shapes=[pltpu.VMEM((HG_HEADS, HG_HEAD_W, HG_HEAD_W), F32)],
        compiler_params=_cparams("parallel", "arbitrary"),
        name="hgrn2",
    )(proj3, proj3, proj3, proj3, lb_logits, norm_w, mst, lv)


ATT_TILE = 2048
ATT_GROUP = 4
ATT_MID = 4
assert ATT_DILATIONS == (1, ATT_MID, ATT_MID * ATT_MID) and ATT_TILE == ATT_SPAN * ATT_DILATIONS[-1]


def _attend_group(qs, ks, vs, first_flags, dist, dist_first, first_head):
    blk = ATT_SPAN
    scores, dists = [], []
    for q2, k2, ff in zip(qs, ks, first_flags):
        qq = jnp.concatenate([jnp.where(first_head, q2, 0.0), jnp.where(first_head, 0.0, q2)], axis=0).astype(BF16)
        scores.append(_dot_nt(qq, k2))
        dsel = jnp.where(ff > 0, dist_first, dist)
        dists.extend([dsel, dsel])
    dall = jnp.concatenate(dists, axis=0)
    valid = (dall >= 0) & (dall <= blk)
    s = jnp.where(valid, jnp.concatenate(scores, axis=0), MASK_VALUE)
    m = jnp.max(s, axis=-1, keepdims=True)
    p = jnp.exp(s - m)
    den = jnp.sum(p, axis=-1, keepdims=True)
    lse = m + jnp.log(den)
    pb = p.astype(BF16)
    outs = []
    for g, v2 in enumerate(vs):
        r0 = g * 2 * blk
        pv = _dot(pb[r0:r0 + 2 * blk], v2)
        d0, d1 = den[r0:r0 + blk], den[r0 + blk:r0 + 2 * blk]
        l0, l1 = lse[r0:r0 + blk], lse[r0 + blk:r0 + 2 * blk]
        o2 = jnp.where(first_head, pv[:blk], pv[blk:]) / jnp.where(first_head, d0, d1)
        outs.append((o2, jnp.where(first_head, l0, l1)))
    return outs


def _attn_kernel(q_ref, k_ref, v_ref, y_ref, q4_ref, k4_ref, v4_ref, kd1, vd1, kd4, vd4, kd16, vd16, op_ref, lp_ref):
    i = pl.program_id(2)
    blk, tile, mid = ATT_SPAN, ATT_TILE, ATT_MID
    len4 = tile // mid
    pitch4 = blk + len4
    pitch16 = 2 * blk
    scale = ATT_HEAD_DIM ** -0.5
    zeros = jnp.zeros((blk, LANES), BF16)

    @pl.when(i == 0)
    def _():
        for kd, vd, pitch, nres in ((kd1, vd1, 0, 1), (kd4, vd4, pitch4, mid), (kd16, vd16, pitch16, mid * mid)):
            for r in range(nres):
                kd[r * pitch:r * pitch + blk, :] = zeros
                vd[r * pitch:r * pitch + blk, :] = zeros

    kd1[blk:blk + tile, :] = k_ref[...].astype(BF16)
    vd1[blk:blk + tile, :] = v_ref[...].astype(BF16)
    for r in range(mid):
        rows = slice(r * len4, (r + 1) * len4)
        cur = slice(r * pitch4 + blk, (r + 1) * pitch4)
        kk = k_ref[pl.ds(r, len4, stride=mid), :]
        vv = v_ref[pl.ds(r, len4, stride=mid), :]
        k4_ref[rows, :] = kk
        v4_ref[rows, :] = vv
        kd4[cur, :] = kk.astype(BF16)
        vd4[cur, :] = vv.astype(BF16)
        q4_ref[rows, :] = q_ref[pl.ds(r, len4, stride=mid), :] * scale
    for r4 in range(mid):
        for hi in range(mid):
            r16 = r4 + mid * hi
            cur = slice(r16 * pitch16 + blk, (r16 + 1) * pitch16)
            kd16[cur, :] = k4_ref[pl.ds(r4 * len4 + hi, blk, stride=mid), :].astype(BF16)
            vd16[cur, :] = v4_ref[pl.ds(r4 * len4 + hi, blk, stride=mid), :].astype(BF16)

    iq = lax.broadcasted_iota(jnp.int32, (blk, 2 * blk), 0)
    ik = lax.broadcasted_iota(jnp.int32, (blk, 2 * blk), 1)
    dist = iq + blk - ik
    dist_first = jnp.where(ik < blk, -1, dist)
    lane = lax.broadcasted_iota(jnp.int32, (1, LANES), 1)
    first_head = lane < ATT_HEAD_DIM
    tile0 = 1 - jnp.minimum(i, 1)
    grp = ATT_GROUP

    def attend(qs, kd, vd, bases, flags):
        ks = [kd[pl.ds(pl.multiple_of(b0, blk), 2 * blk), :] for b0 in bases]
        vs = [vd[pl.ds(pl.multiple_of(b0, blk), 2 * blk), :] for b0 in bases]
        return _attend_group(qs, ks, vs, flags, dist, dist_first, first_head)

    def pattern1(it, carry):
        b0s = [(it * grp + g) * blk for g in range(grp)]
        qs = [q_ref[pl.ds(pl.multiple_of(b0, blk), blk), :] * scale for b0 in b0s]
        flags = [tile0 * (1 - jnp.minimum(it * grp + g, 1)) for g in range(grp)]
        for b0, (o2, l2) in zip(b0s, attend(qs, kd1, vd1, b0s, flags)):
            op_ref[0, pl.ds(pl.multiple_of(b0, blk), blk), :] = o2
            lp_ref[0, pl.ds(pl.multiple_of(b0, blk), blk), :] = l2
        return carry

    def pattern4(r, carry):
        qs = [q4_ref[pl.ds(pl.multiple_of(r * len4 + g * blk, blk), blk), :] for g in range(grp)]
        bases = [r * pitch4 + g * blk for g in range(grp)]
        flags = [tile0 if g == 0 else 0 * tile0 for g in range(grp)]
        for g, (o2, l2) in enumerate(attend(qs, kd4, vd4, bases, flags)):
            op_ref[1, pl.ds(r + mid * blk * g, blk, stride=mid), :] = o2
            lp_ref[1, pl.ds(r + mid * blk * g, blk, stride=mid), :] = l2
        return carry

    def pattern16(hi, carry):
        qs = [q4_ref[pl.ds(g * len4 + hi, blk, stride=mid), :] for g in range(grp)]
        bases = [(g + mid * hi) * pitch16 for g in range(grp)]
        flags = [tile0] * grp
        for g, (o2, l2) in enumerate(attend(qs, kd16, vd16, bases, flags)):
            op_ref[2, pl.ds(g + mid * hi, blk, stride=mid * mid), :] = o2
            lp_ref[2, pl.ds(g + mid * hi, blk, stride=mid * mid), :] = l2
        return carry

    assert grp == mid and tile // blk == grp * mid
    lax.fori_loop(0, mid, pattern1, 0)
    lax.fori_loop(0, mid, pattern4, 0)
    lax.fori_loop(0, mid, pattern16, 0)

    for kd, vd, pitch, nres, ln in ((kd1, vd1, 0, 1, tile), (kd4, vd4, pitch4, mid, len4),
                                    (kd16, vd16, pitch16, mid * mid, blk)):
        for r in range(nres):
            kd[r * pitch:r * pitch + blk, :] = kd[r * pitch + ln:r * pitch + ln + blk, :]
            vd[r * pitch:r * pitch + blk, :] = vd[r * pitch + ln:r * pitch + ln + blk, :]

    def merge(c, carry):
        rows = pl.ds(pl.multiple_of(c * CHUNK, CHUNK), CHUNK)
        la, lb, lc = lp_ref[0, rows, :], lp_ref[1, rows, :], lp_ref[2, rows, :]
        m = jnp.maximum(jnp.maximum(la, lb), lc)
        wa, wb, wc = jnp.exp(la - m), jnp.exp(lb - m), jnp.exp(lc - m)
        y_ref[rows, :] = (wa * op_ref[0, rows, :] + wb * op_ref[1, rows, :] + wc * op_ref[2, rows, :]) / (wa + wb + wc)
        return carry

    lax.fori_loop(0, tile // CHUNK, merge, 0)


def _attn(proj3):
    b, s, _ = proj3.shape
    tile, blk, mid = ATT_TILE, ATT_SPAN, ATT_MID
    npair = GROUP_W // LANES
    per = GROUP_W // LANES
    spec = lambda off: pl.BlockSpec((None, tile, LANES), lambda bi, pr, ti: (bi, ti, (BLK_B + off) * per + pr))
    return pl.pallas_call(
        _attn_kernel,
        grid=(b, npair, s // tile),
        in_specs=[spec(0), spec(1), spec(2)],
        out_specs=pl.BlockSpec((None, tile, LANES), lambda bi, pr, ti: (bi, ti, pr)),
        out_shape=jax.ShapeDtypeStruct((b, s, GROUP_W), F32),
        scratch_shapes=[pltpu.VMEM((tile, LANES), F32)] * 3
        + [pltpu.VMEM((blk + tile, LANES), BF16)] * 2
        + [pltpu.VMEM((mid * (blk + tile // mid), LANES), BF16)] * 2
        + [pltpu.VMEM((mid * mid * 2 * blk, LANES), BF16)] * 2
        + [pltpu.VMEM((len(ATT_DILATIONS), tile, LANES), F32)] * 2,
        compiler_params=_cparams("parallel", "parallel", "arbitrary"),
        name="dilattn",
    )(proj3, proj3, proj3)


def _causal_conv(x_ref, xpad_ref, w_ref, b_ref, first):
    c = x_ref.shape[0]

    @pl.when(first)
    def _():
        xpad_ref[0:SUBLANES, :] = jnp.zeros((SUBLANES, xpad_ref.shape[1]), F32)

    xpad_ref[SUBLANES:SUBLANES + c, :] = x_ref[...]
    acc = b_ref[...] + w_ref[0:1, :] * xpad_ref[pl.ds(SUBLANES - CONV_W + 1, c), :]
    for j in range(1, CONV_W):
        acc = acc + w_ref[j:j + 1, :] * xpad_ref[pl.ds(SUBLANES - CONV_W + 1 + j, c), :]
    xpad_ref[0:SUBLANES, :] = xpad_ref[c:c + SUBLANES, :]
    return acc


def _pair_lanes(col_fn, first_head):
    return jnp.where(first_head, col_fn(0), col_fn(1))


def _ssd_kernel(z_ref, xbc_ref, dt_ref, cw_ref, cb_ref, dtb_ref, alog_ref, dsk_ref, nw_ref, tril_ref,
                y_ref, xpad_ref, st_ref):
    c = CHUNK
    first = pl.program_id(1) == 0

    @pl.when(first)
    def _():
        st_ref[...] = jnp.zeros_like(st_ref)

    xbc = _silu(_causal_conv(xbc_ref, xpad_ref, cw_ref, cb_ref, first))
    dt = _softplus(dt_ref[...] + dtb_ref[...])
    adt = dt * (-jnp.exp(alog_ref[...]))
    tril01 = tril_ref[...]
    acs = _sum01_matmul(tril01, adt)
    acs_t = acs.T
    e_cs = jnp.exp(acs)
    dt_rest = dt * jnp.exp(acs[c - 1:c, :] - acs)
    causal = tril01 > 0

    lane = lax.broadcasted_iota(jnp.int32, (1, LANES), 1)
    first_head = lane < SSM_HEAD_DIM
    gw = GROUP_W // SSM_GROUPS
    hpg = SSM_HEADS // SSM_GROUPS
    for g in range(SSM_GROUPS):
        bg = xbc[:, GROUP_W + g * SSM_STATE:GROUP_W + (g + 1) * SSM_STATE]
        cg = xbc[:, GROUP_W + (SSM_GROUPS + g) * SSM_STATE:GROUP_W + (SSM_GROUPS + g + 1) * SSM_STATE].astype(BF16)
        cb = _dot_nt(cg, bg.astype(BF16))
        st = st_ref[g]
        y_off = _dot(cg, st.astype(BF16))
        ys, xds, dec_last = [], [], []
        for pr in range(hpg // 2):
            h0 = g * hpg + 2 * pr
            ls = slice(h0 * SSM_HEAD_DIM, (h0 + 2) * SSM_HEAD_DIM)
            xs = xbc[:, ls]
            xdt = (xs * _pair_lanes(lambda i: dt[:, h0 + i:h0 + i + 1], first_head)).astype(BF16)
            yd = []
            for i in range(2):
                hh = h0 + i
                seg = acs[:, hh:hh + 1] - acs_t[hh:hh + 1, :]
                lmat = jnp.where(causal, jnp.exp(jnp.where(causal, seg, 0.0)), 0.0)
                yd.append(_dot((cb * lmat).astype(BF16), xdt))
            y_pair = (jnp.where(first_head, yd[0], yd[1])
                      + y_off[:, pr * LANES:(pr + 1) * LANES] * _pair_lanes(lambda i: e_cs[:, h0 + i:h0 + i + 1], first_head)
                      + dsk_ref[:, ls] * xs)
            ys.append(y_pair)
            xds.append((xs * _pair_lanes(lambda i: dt_rest[:, h0 + i:h0 + i + 1], first_head)).astype(BF16))
            dec_last.append(_pair_lanes(lambda i: e_cs[c - 1:c, h0 + i:h0 + i + 1], first_head))
        st_ref[g] = (jnp.concatenate(dec_last, axis=1) * st
                     + _dot_tn(bg.astype(BF16), jnp.concatenate(xds, axis=1)))
        yg = jnp.concatenate(ys, axis=1) * _silu(z_ref[:, g * gw:(g + 1) * gw])
        y_ref[:, g * gw:(g + 1) * gw] = _rms(yg, nw_ref[:, g * gw:(g + 1) * gw]).astype(y_ref.dtype)


def _ssd(proj3, conv_w, conv_b, dt_bias, a_log, d_skip, norm_w):
    b, s, _ = proj3.shape
    full = lambda shape: pl.BlockSpec(shape, lambda bi, ci: (0,) * len(shape))
    tril = jnp.asarray(np.tril(np.ones((CHUNK, CHUNK), np.float32)), BF16)
    return pl.pallas_call(
        _ssd_kernel,
        grid=(b, s // CHUNK),
        in_specs=[
            pl.BlockSpec((None, CHUNK, GROUP_W), lambda bi, ci: (bi, ci, BLK_CZ)),
            pl.BlockSpec((None, CHUNK, SSM_CONV_DIM), lambda bi, ci: (bi, ci, BLK_CXBC * GROUP_W // SSM_CONV_DIM)),
            pl.BlockSpec((None, CHUNK, LANES), lambda bi, ci: (bi, ci, BLK_CDT * GROUP_W // LANES)),
            full((CONV_W, SSM_CONV_DIM)), full((1, SSM_CONV_DIM)), full((1, LANES)), full((1, LANES)),
            full((1, GROUP_W)), full((1, GROUP_W)), full((CHUNK, CHUNK)),
        ],
        out_specs=pl.BlockSpec((None, CHUNK, GROUP_W), lambda bi, ci: (bi, ci, 0)),
        out_shape=jax.ShapeDtypeStruct((b, s, GROUP_W), BF16),
        scratch_shapes=[pltpu.VMEM((CHUNK + SUBLANES, SSM_CONV_DIM), F32),
                        pltpu.VMEM((SSM_GROUPS, SSM_STATE, GROUP_W // SSM_GROUPS), F32)],
        compiler_params=_cparams("parallel", "arbitrary"),
        name="ssd",
    )(proj3, proj3, proj3, conv_w, conv_b, dt_bias, a_log, d_skip, norm_w, tril)


def _gelu_tanh(x):
    return 0.5 * x * (1.0 + jnp.tanh(np.sqrt(2.0 / np.pi).astype(np.float32) * (x + 0.044715 * (x * x * x))))


def _lru_kernel(x_ref, g_ref, cw_ref, cb_ref, wa_ref, ba_ref, wx_ref, bx_ref, ap_ref, nw_ref,
                y_ref, xpad_ref, h_ref):
    c = CHUNK
    first = pl.program_id(1) == 0

    @pl.when(first)
    def _():
        h_ref[...] = jnp.zeros_like(h_ref)

    xc = _causal_conv(x_ref, xpad_ref, cw_ref, cb_ref, first)
    xb = xc.astype(BF16)
    r = jax.nn.sigmoid(_dot(xb, wa_ref[...]) + ba_ref[...])
    i = jax.nn.sigmoid(_dot(xb, wx_ref[...]) + bx_ref[...])
    log_a = -LRU_C * r * _softplus(-ap_ref[...])
    a = jnp.exp(log_a)
    bt = jnp.sqrt(jnp.maximum(-_expm1(2.0 * log_a), 0.0)) * (i * xc)
    row = lax.broadcasted_iota(jnp.int32, (c, 1), 0)
    d = 1
    while d < c:
        keep = row >= d
        a_s = pltpu.roll(a, d, 0)
        b_s = pltpu.roll(bt, d, 0)
        bt = jnp.where(keep, a * b_s + bt, bt)
        a = jnp.where(keep, a * a_s, a)
        d *= 2
    h = bt + a * h_ref[...]
    h_ref[...] = h[c - 1:c, :]
    y_ref[...] = _rms(h * _gelu_tanh(g_ref[...]), nw_ref[...]).astype(y_ref.dtype)


def _lru(proj3, conv_w, conv_b, wa_bd, b_a, wx_bd, b_x, a_param, norm_w):
    b, s, _ = proj3.shape
    full = lambda shape: pl.BlockSpec(shape, lambda bi, ci: (0,) * len(shape))
    vec = full((1, GROUP_W))
    return pl.pallas_call(
        _lru_kernel,
        grid=(b, s // CHUNK),
        in_specs=[
            pl.BlockSpec((None, CHUNK, GROUP_W), lambda bi, ci: (bi, ci, BLK_DX)),
            pl.BlockSpec((None, CHUNK, GROUP_W), lambda bi, ci: (bi, ci, BLK_DG)),
            full((CONV_W, GROUP_W)), vec, full((GROUP_W, GROUP_W)), vec, full((GROUP_W, GROUP_W)), vec, vec, vec,
        ],
        out_specs=pl.BlockSpec((None, CHUNK, GROUP_W), lambda bi, ci: (bi, ci, 0)),
        out_shape=jax.ShapeDtypeStruct((b, s, GROUP_W), BF16),
        scratch_shapes=[pltpu.VMEM((CHUNK + SUBLANES, GROUP_W), F32), pltpu.VMEM((1, GROUP_W), F32)],
        compiler_params=_cparams("parallel", "arbitrary"),
        name="rglru",
    )(proj3, proj3, conv_w, conv_b, wa_bd, b_a, wx_bd, b_x, a_param, norm_w)


def _block_diag(w):
    nb, n, _ = w.shape
    eye = jnp.eye(nb, dtype=w.dtype)
    return jnp.einsum("hij,hg->higj", w, eye).reshape(nb * n, nb * n)


def _pad_lanes(v, width=LANES):
    return jnp.pad(v, (0, width - v.shape[0])).reshape(1, width)


def kernel(x, ffn1_norm, ffn1_w_gate, ffn1_w_up, ffn1_w_down, mix_norm, w_in, w_out, hgrn_lb_logits, hgrn_norm, attn_norm, ssm_conv_w, ssm_conv_b, ssm_dt_bias, ssm_a_log, ssm_d, ssm_norm, lru_conv_w, lru_conv_b, lru_w_a, lru_b_a, lru_w_x, lru_b_x, lru_a_param, lru_norm, ffn2_norm, ffn2_w_gate, ffn2_w_up, ffn2_w_down, final_norm):
    bsz, seq, d = x.shape
    depth = w_in.shape[0]
    t = bsz * seq
    row = lambda v: v.reshape(1, -1).astype(F32)
    x2 = x.reshape(t, d)
    lb_logits = hgrn_lb_logits.astype(F32)
    for l in range(depth):
        x2 = _ffn(x2, row(ffn1_norm[l]), ffn1_w_gate[l].astype(BF16), ffn1_w_up[l].astype(BF16),
                  ffn1_w_down[l].astype(BF16))
        w_in_l = w_in[l]
        w_pad = jnp.concatenate(
            [w_in_l[:, :N_IN_BEFORE_PAD],
             jnp.zeros((d, (BLK_DX * GROUP_W) - N_IN_BEFORE_PAD), w_in_l.dtype),
             w_in_l[:, N_IN_BEFORE_PAD:]], axis=1).astype(BF16)
        proj3 = _inproj(x2, row(mix_norm[l]), w_pad).reshape(bsz, seq, PROJ_W)
        y_a = _hgrn(proj3, lb_logits, row(hgrn_norm[l]), layer=l).reshape(t, GROUP_W)
        y_b = _attn(proj3).reshape(t, GROUP_W)
        y_c = _ssd(proj3, ssm_conv_w[l].astype(F32), row(ssm_conv_b[l]), _pad_lanes(ssm_dt_bias[l].astype(F32)),
                   _pad_lanes(ssm_a_log[l].astype(F32)), row(jnp.repeat(ssm_d[l], SSM_HEAD_DIM)),
                   row(ssm_norm[l])).reshape(t, GROUP_W)
        y_d = _lru(proj3, lru_conv_w[l].astype(F32), row(lru_conv_b[l]), _block_diag(lru_w_a[l]).astype(BF16),
                   row(lru_b_a[l]), _block_diag(lru_w_x[l]).astype(BF16), row(lru_b_x[l]), row(lru_a_param[l]),
                   row(lru_norm[l])).reshape(t, GROUP_W)
        x2 = _outproj(x2, y_a, y_b, y_c, y_d, row(attn_norm[l]), w_out[l].astype(BF16))
        x2 = _ffn(x2, row(ffn2_norm[l]), ffn2_w_gate[l].astype(BF16), ffn2_w_up[l].astype(BF16),
                  ffn2_w_down[l].astype(BF16))
    return _final_norm(x2, row(final_norm)).reshape(bsz, seq, d)
```

```python
import functools

import numpy as np
import jax
import jax.numpy as jnp
from jax import lax
from jax.experimental import pallas as pl
from jax.experimental.pallas import tpu as pltpu

F32 = jnp.float32
BF16 = jnp.bfloat16

NORM_EPS = 1e-6
MASK_VALUE = -1e30
GROUP_W = 512
HG_HEADS = 4
HG_HEAD_W = GROUP_W // HG_HEADS
ATT_HEAD_DIM = 64
ATT_SPAN = 128
ATT_DILATIONS = (1, 4, 16)
SSM_HEADS = 8
SSM_HEAD_DIM = 64
SSM_GROUPS = 2
SSM_STATE = 128
SSM_CONV_DIM = GROUP_W + 2 * SSM_GROUPS * SSM_STATE
CONV_W = 4
LRU_C = 8.0

LANES = 128
SUBLANES = 8
VMEM_LIMIT = 52 * 1024 * 1024

PROJ_BLOCKS = 13
PROJ_W = PROJ_BLOCKS * GROUP_W
BLK_A = 0
BLK_B = 4
BLK_CZ = 7
BLK_CXBC = 8
BLK_CDT = 10
BLK_DX = 11
BLK_DG = 12
N_IN_BEFORE_PAD = 10 * GROUP_W + SSM_HEADS
SLABS = GROUP_W // LANES

CHUNK = 256
CHUNK_LEVELS = 8


def _cparams(*sem):
    return pltpu.CompilerParams(dimension_semantics=sem, vmem_limit_bytes=VMEM_LIMIT)


def _rms(x, w):
    return x * lax.rsqrt(jnp.mean(x * x, axis=-1, keepdims=True) + NORM_EPS) * w


def _silu(x):
    return x * jax.nn.sigmoid(x)


def _softplus(x):
    return jnp.maximum(x, 0.0) + jnp.log1p(jnp.exp(-jnp.abs(x)))


def _expm1(u):
    w = jnp.exp(u)
    near = (jnp.abs(u) < 0.5) & (w != 1.0)
    kahan = (w - 1.0) * u / jnp.log(jnp.where(near, w, 2.0))
    return jnp.where(near, kahan, jnp.where(w == 1.0, u, w - 1.0))


def _dot(a, b):
    return jnp.dot(a, b, preferred_element_type=F32)


def _dot_nt(a, b):
    return lax.dot_general(a, b, (((1,), (1,)), ((), ())), preferred_element_type=F32)


def _dot_tn(a, b):
    return lax.dot_general(a, b, (((0,), (0,)), ((), ())), preferred_element_type=F32)


def _sum01_matmul(m01, x):
    hi = x.astype(BF16)
    r1 = x - hi.astype(F32)
    mid = r1.astype(BF16)
    lo = (r1 - mid.astype(F32)).astype(BF16)
    return _dot(m01, hi) + _dot(m01, mid) + _dot(m01, lo)


def _sum01_matmul2(m01, x):
    hi = x.astype(BF16)
    mid = (x - hi.astype(F32)).astype(BF16)
    both = _dot(m01, jnp.concatenate([hi, mid], axis=1))
    n = x.shape[1]
    return both[:, :n] + both[:, n:]


def _ffn_kernel(x_ref, xc_ref, nw_ref, wg_ref, wu_ref, wd_ref, o_ref, h_ref, a_ref, *, n_f, tf):
    j = pl.program_id(1)

    @pl.when(j == 0)
    def _():
        h_ref[...] = _rms(x_ref[...], nw_ref[...]).astype(BF16)

    @pl.when(j < n_f)
    def _():
        h = h_ref[...]
        a_ref[j] = (_silu(_dot(h, wg_ref[...])) * _dot(h, wu_ref[...])).astype(BF16)

    @pl.when(j >= n_f)
    def _():
        acc = _dot(a_ref[0], wd_ref[0:tf, :])
        for jj in range(1, n_f):
            acc = acc + _dot(a_ref[jj], wd_ref[jj * tf:(jj + 1) * tf, :])
        o_ref[...] = xc_ref[...] + 0.5 * acc


FFN_TM = 1024
FFN_TF = 512
FFN_TN = 256


def _tile_cols(w, tile):
    nl, k, n = w.shape
    return w.astype(BF16).reshape(nl, k, n // tile, tile).transpose(0, 2, 1, 3)


def _ffn(x2, nw, wg_t, wu_t, wd_t, layer, *, tm=FFN_TM):
    t, d = x2.shape
    _, n_f, _, tf = wg_t.shape
    _, n_d, f, tn = wd_t.shape
    up = lambda i, j: (layer, jnp.minimum(j, n_f - 1), 0, 0)
    down = lambda i, j: (layer, jnp.maximum(j - n_f, 0), 0, 0)
    col = lambda i, j: (i, jnp.maximum(j - n_f, 0))
    return pl.pallas_call(
        functools.partial(_ffn_kernel, n_f=n_f, tf=tf),
        grid=(t // tm, n_f + n_d),
        in_specs=[
            pl.BlockSpec((tm, d), lambda i, j: (i, 0), pipeline_mode=pl.Buffered(1)),
            pl.BlockSpec((tm, tn), col),
            pl.BlockSpec((1, d), lambda i, j: (0, 0)),
            pl.BlockSpec((None, None, d, tf), up),
            pl.BlockSpec((None, None, d, tf), up),
            pl.BlockSpec((None, None, f, tn), down),
        ],
        out_specs=pl.BlockSpec((tm, tn), col),
        out_shape=jax.ShapeDtypeStruct((t, d), F32),
        scratch_shapes=[pltpu.VMEM((tm, d), BF16), pltpu.VMEM((n_f, tm, tf), BF16)],
        compiler_params=_cparams("parallel", "arbitrary"),
        name="ffn",
    )(x2, x2, nw, wg_t, wu_t, wd_t)


def _final_norm_kernel(x_ref, w_ref, o_ref):
    o_ref[...] = _rms(x_ref[...], w_ref[...])


def _final_norm(x2, w, *, tm=1024):
    t, d = x2.shape
    return pl.pallas_call(
        _final_norm_kernel,
        grid=(t // tm,),
        in_specs=[pl.BlockSpec((tm, d), lambda i: (i, 0)), pl.BlockSpec((1, d), lambda i: (0, 0))],
        out_specs=pl.BlockSpec((tm, d), lambda i: (i, 0)),
        out_shape=jax.ShapeDtypeStruct((t, d), F32),
        compiler_params=_cparams("parallel"),
        name="final_norm",
    )(x2, w)


def _inproj_kernel(x_ref, nw_ref, w_ref, o_ref, h_ref):
    @pl.when(pl.program_id(1) == 0)
    def _():
        h_ref[...] = _rms(x_ref[...], nw_ref[...]).astype(BF16)

    res = _dot(h_ref[...], w_ref[...])
    for p in range(SLABS):
        o_ref[p] = res[:, p * LANES:(p + 1) * LANES]


def _inproj(x2, nw, w_t, layer, *, tm=1024):
    t, d = x2.shape
    return pl.pallas_call(
        _inproj_kernel,
        grid=(t // tm, PROJ_BLOCKS),
        in_specs=[
            pl.BlockSpec((tm, d), lambda i, j: (i, 0)),
            pl.BlockSpec((1, d), lambda i, j: (0, 0)),
            pl.BlockSpec((None, None, d, GROUP_W), lambda i, j: (layer, j, 0, 0)),
        ],
        out_specs=pl.BlockSpec((SLABS, tm, LANES), lambda i, j: (j, i, 0)),
        out_shape=jax.ShapeDtypeStruct((PROJ_BLOCKS * SLABS, t, LANES), F32),
        scratch_shapes=[pltpu.VMEM((tm, d), BF16)],
        compiler_params=_cparams("parallel", "arbitrary"),
        name="inproj",
    )(x2, nw, w_t)


def _outproj_kernel(x_ref, ya_ref, yb_ref, yc_ref, yd_ref, bnw_ref, w_ref, o_ref):
    yb = _rms(yb_ref[...], bnw_ref[...]).astype(BF16)
    acc = x_ref[...]
    for g, y in enumerate((ya_ref[...], yb, yc_ref[...], yd_ref[...])):
        acc = acc + _dot(y, w_ref[g * GROUP_W:(g + 1) * GROUP_W, :])
    o_ref[...] = acc


def _outproj(x2, ya, yb, yc, yd, attn_norm_w, w, *, tm=512):
    t, d = x2.shape
    yspec = pl.BlockSpec((tm, GROUP_W), lambda i: (i, 0))
    return pl.pallas_call(
        _outproj_kernel,
        grid=(t // tm,),
        in_specs=[pl.BlockSpec((tm, d), lambda i: (i, 0)), yspec, yspec, yspec, yspec,
                  pl.BlockSpec((1, GROUP_W), lambda i: (0, 0)), pl.BlockSpec((4 * GROUP_W, d), lambda i: (0, 0))],
        out_specs=pl.BlockSpec((tm, d), lambda i: (i, 0)),
        out_shape=jax.ShapeDtypeStruct((t, d), F32),
        compiler_params=_cparams("parallel"),
        name="outproj",
    )(x2, ya, yb, yc, yd, attn_norm_w, w)


def _hgrn_tables(c=CHUNK, levels=CHUNK_LEVELS):
    t = np.arange(c)
    rows = []
    for lvl in range(levels):
        h = 1 << lvl
        r = (t // (2 * h)) * (2 * h) + h
        up = (t // h) % 2 == 1
        u = t[None, :]
        m = np.where(up[:, None], (u > r[:, None]) & (u <= t[:, None]), (u > t[:, None]) & (u <= r[:, None]))
        rows.append(m)
    rows.append(t[None, :] <= t[:, None])
    rows.append(t[None, :] > t[:, None])
    mstack = np.concatenate(rows, axis=0).astype(np.float32)
    x = t[:, None] ^ t[None, :]
    lv = np.where(t[None, :] < t[:, None], np.floor(np.log2(np.maximum(x, 1))).astype(np.int32), -1)
    return mstack, lv.astype(np.int32)


_HGRN_MSTACK, _HGRN_LEVEL = _hgrn_tables()


def _hgrn_kernel(q_ref, f_ref, i_ref, g_ref, lbl_ref, nw_ref, mst_ref, lv_ref, o_ref, st_ref, *, layer):
    c = CHUNK

    @pl.when(pl.program_id(1) == 0)
    def _():
        st_ref[...] = jnp.zeros_like(st_ref)

    lbl = lbl_ref[...]
    e = jnp.exp(lbl - jnp.max(lbl, axis=0, keepdims=True))
    p = e / jnp.sum(e, axis=0, keepdims=True)
    lb = jnp.zeros((1, GROUP_W), F32)
    for i in range(1, layer + 1):
        lb = lb + p[i:i + 1, :]

    mst = mst_ref[...]
    lv = lv_ref[...]
    row = lax.broadcasted_iota(jnp.int32, (c, 1), 0)
    for h in range(HG_HEADS):
        sl = slice(h * HG_HEAD_W, (h + 1) * HG_HEAD_W)
        q = _silu(q_ref[h])
        z = f_ref[h]
        v = i_ref[h]
        lbh = lb[:, sl]
        g = jnp.log(lbh + (1.0 - lbh) * jax.nn.sigmoid(z))
        k = (1.0 - lbh) * jax.nn.sigmoid(-z)
        ex = jnp.exp(_sum01_matmul2(mst, g))
        a = jnp.zeros((c, c), F32)
        for lvl in range(CHUNK_LEVELS):
            exl = ex[lvl * c:(lvl + 1) * c]
            up = ((row >> lvl) & 1) == 1
            qp = jnp.where(up, q * exl, 0.0).astype(BF16)
            kp = jnp.where(up, 0.0, k * exl).astype(BF16)
            a = jnp.where(lv == lvl, _dot_nt(qp, kp), a)
        vb = v.astype(BF16)
        ex_b = ex[CHUNK_LEVELS * c:(CHUNK_LEVELS + 1) * c]
        ex_r = ex[(CHUNK_LEVELS + 1) * c:(CHUNK_LEVELS + 2) * c]
        st = st_ref[h]
        o = (_dot(a.astype(BF16), vb) + jnp.sum(q * k, axis=-1, keepdims=True) * v
             + _dot_nt((q * ex_b).astype(BF16), st.astype(BF16)))
        st_ref[h] = ex_b[c - 1:c, :] * st + _dot_tn(vb, (k * ex_r).astype(BF16))
        o = _rms(o, nw_ref[:, sl]) * _silu(g_ref[h])
        o_ref[:, sl] = o.astype(o_ref.dtype)


def _slab_spec(blk, rows, n=SLABS):
    first = blk * SLABS // n
    return pl.BlockSpec((n, None, rows, LANES), lambda bi, ci: (first, bi, ci, 0))


def _hgrn(proj4, lb_logits, norm_w, *, layer):
    _, b, s, _ = proj4.shape
    cspec = lambda blk: _slab_spec(blk, CHUNK)
    full = lambda shape: pl.BlockSpec(shape, lambda bi, ci: (0,) * len(shape))
    mst = jnp.asarray(_HGRN_MSTACK, BF16)
    lv = jnp.asarray(_HGRN_LEVEL)
    return pl.pallas_call(
        functools.partial(_hgrn_kernel, layer=layer),
        grid=(b, s // CHUNK),
        in_specs=[cspec(BLK_A), cspec(BLK_A + 1), cspec(BLK_A + 2), cspec(BLK_A + 3),
                  full(lb_logits.shape), full((1, GROUP_W)), full(mst.shape), full(lv.shape)],
        out_specs=pl.BlockSpec((None, CHUNK, GROUP_W), lambda bi, ci: (bi, ci, 0)),
        out_shape=jax.ShapeDtypeStruct((b, s, GROUP_W), BF16),
        scratch_shapes=[pltpu.VMEM((HG_HEADS, HG_HEAD_W, HG_HEAD_W), F32)],
        compiler_params=_cparams("parallel", "arbitrary"),
        name="hgrn2",
    )(proj4, proj4, proj4, proj4, lb_logits, norm_w, mst, lv)


ATT_TILE = 2048
ATT_GROUP = 4
ATT_MID = 4
assert ATT_DILATIONS == (1, ATT_MID, ATT_MID * ATT_MID) and ATT_TILE == ATT_SPAN * ATT_DILATIONS[-1]


def _attend_group(qs, ks, vs, first_flags, dist, dist_first, first_head):
    blk = ATT_SPAN
    scores, dists = [], []
    for q2, k2, ff in zip(qs, ks, first_flags):
        qq = jnp.concatenate([jnp.where(first_head, q2, 0.0), jnp.where(first_head, 0.0, q2)], axis=0).astype(BF16)
        scores.append(_dot_nt(qq, k2))
        dsel = jnp.where(ff > 0, dist_first, dist)
        dists.extend([dsel, dsel])
    dall = jnp.concatenate(dists, axis=0)
    valid = (dall >= 0) & (dall <= blk)
    s = jnp.where(valid, jnp.concatenate(scores, axis=0), MASK_VALUE)
    m = jnp.max(s, axis=-1, keepdims=True)
    p = jnp.exp(s - m)
    den = jnp.sum(p, axis=-1, keepdims=True)
    lse = m + jnp.log(den)
    pb = p.astype(BF16)
    outs = []
    for g, v2 in enumerate(vs):
        r0 = g * 2 * blk
        pv = _dot(pb[r0:r0 + 2 * blk], v2)
        d0, d1 = den[r0:r0 + blk], den[r0 + blk:r0 + 2 * blk]
        l0, l1 = lse[r0:r0 + blk], lse[r0 + blk:r0 + 2 * blk]
        o2 = jnp.where(first_head, pv[:blk], pv[blk:]) / jnp.where(first_head, d0, d1)
        outs.append((o2, jnp.where(first_head, l0, l1)))
    return outs


def _attn_kernel(q_ref, k_ref, v_ref, y_ref, q4_ref, k4_ref, v4_ref, kd1, vd1, kd4, vd4, kd16, vd16, op_ref, lp_ref):
    i = pl.program_id(2)
    blk, tile, mid = ATT_SPAN, ATT_TILE, ATT_MID
    len4 = tile // mid
    pitch4 = blk + len4
    pitch16 = 2 * blk
    scale = ATT_HEAD_DIM ** -0.5
    zeros = jnp.zeros((blk, LANES), BF16)

    @pl.when(i == 0)
    def _():
        for kd, vd, pitch, nres in ((kd1, vd1, 0, 1), (kd4, vd4, pitch4, mid), (kd16, vd16, pitch16, mid * mid)):
            for r in range(nres):
                kd[r * pitch:r * pitch + blk, :] = zeros
                vd[r * pitch:r * pitch + blk, :] = zeros

    kd1[blk:blk + tile, :] = k_ref[...].astype(BF16)
    vd1[blk:blk + tile, :] = v_ref[...].astype(BF16)
    for r in range(mid):
        rows = slice(r * len4, (r + 1) * len4)
        cur = slice(r * pitch4 + blk, (r + 1) * pitch4)
        kk = k_ref[pl.ds(r, len4, stride=mid), :]
        vv = v_ref[pl.ds(r, len4, stride=mid), :]
        k4_ref[rows, :] = kk
        v4_ref[rows, :] = vv
        kd4[cur, :] = kk.astype(BF16)
        vd4[cur, :] = vv.astype(BF16)
        q4_ref[rows, :] = q_ref[pl.ds(r, len4, stride=mid), :] * scale
    for r4 in range(mid):
        for hi in range(mid):
            r16 = r4 + mid * hi
            cur = slice(r16 * pitch16 + blk, (r16 + 1) * pitch16)
            kd16[cur, :] = k4_ref[pl.ds(r4 * len4 + hi, blk, stride=mid), :].astype(BF16)
            vd16[cur, :] = v4_ref[pl.ds(r4 * len4 + hi, blk, stride=mid), :].astype(BF16)

    iq = lax.broadcasted_iota(jnp.int32, (blk, 2 * blk), 0)
    ik = lax.broadcasted_iota(jnp.int32, (blk, 2 * blk), 1)
    dist = iq + blk - ik
    dist_first = jnp.where(ik < blk, -1, dist)
    lane = lax.broadcasted_iota(jnp.int32, (1, LANES), 1)
    first_head = lane < ATT_HEAD_DIM
    tile0 = 1 - jnp.minimum(i, 1)
    grp = ATT_GROUP

    def attend(qs, kd, vd, bases, flags):
        ks = [kd[pl.ds(pl.multiple_of(b0, blk), 2 * blk), :] for b0 in bases]
        vs = [vd[pl.ds(pl.multiple_of(b0, blk), 2 * blk), :] for b0 in bases]
        return _attend_group(qs, ks, vs, flags, dist, dist_first, first_head)

    def pattern1(it, carry):
        b0s = [(it * grp + g) * blk for g in range(grp)]
        qs = [q_ref[pl.ds(pl.multiple_of(b0, blk), blk), :] * scale for b0 in b0s]
        flags = [tile0 * (1 - jnp.minimum(it * grp + g, 1)) for g in range(grp)]
        for b0, (o2, l2) in zip(b0s, attend(qs, kd1, vd1, b0s, flags)):
            op_ref[0, pl.ds(pl.multiple_of(b0, blk), blk), :] = o2
            lp_ref[0, pl.ds(pl.multiple_of(b0, blk), blk), :] = l2
        return carry

    def pattern4(r, carry):
        qs = [q4_ref[pl.ds(pl.multiple_of(r * len4 + g * blk, blk), blk), :] for g in range(grp)]
        bases = [r * pitch4 + g * blk for g in range(grp)]
        flags = [tile0 if g == 0 else 0 * tile0 for g in range(grp)]
        for g, (o2, l2) in enumerate(attend(qs, kd4, vd4, bases, flags)):
            op_ref[1, pl.ds(r + mid * blk * g, blk, stride=mid), :] = o2
            lp_ref[1, pl.ds(r + mid * blk * g, blk, stride=mid), :] = l2
        return carry

    def pattern16(hi, carry):
        qs = [q4_ref[pl.ds(g * len4 + hi, blk, stride=mid), :] for g in range(grp)]
        bases = [(g + mid * hi) * pitch16 for g in range(grp)]
        flags = [tile0] * grp
        for g, (o2, l2) in enumerate(attend(qs, kd16, vd16, bases, flags)):
            op_ref[2, pl.ds(g + mid * hi, blk, stride=mid * mid), :] = o2
            lp_ref[2, pl.ds(g + mid * hi, blk, stride=mid * mid), :] = l2
        return carry

    assert grp == mid and tile // blk == grp * mid
    lax.fori_loop(0, mid, pattern1, 0)
    lax.fori_loop(0, mid, pattern4, 0)
    lax.fori_loop(0, mid, pattern16, 0)

    for kd, vd, pitch, nres, ln in ((kd1, vd1, 0, 1, tile), (kd4, vd4, pitch4, mid, len4),
                                    (kd16, vd16, pitch16, mid * mid, blk)):
        for r in range(nres):
            kd[r * pitch:r * pitch + blk, :] = kd[r * pitch + ln:r * pitch + ln + blk, :]
            vd[r * pitch:r * pitch + blk, :] = vd[r * pitch + ln:r * pitch + ln + blk, :]

    def merge(c, carry):
        rows = pl.ds(pl.multiple_of(c * CHUNK, CHUNK), CHUNK)
        la, lb, lc = lp_ref[0, rows, :], lp_ref[1, rows, :], lp_ref[2, rows, :]
        m = jnp.maximum(jnp.maximum(la, lb), lc)
        wa, wb, wc = jnp.exp(la - m), jnp.exp(lb - m), jnp.exp(lc - m)
        y_ref[rows, :] = (wa * op_ref[0, rows, :] + wb * op_ref[1, rows, :] + wc * op_ref[2, rows, :]) / (wa + wb + wc)
        return carry

    lax.fori_loop(0, tile // CHUNK, merge, 0)


def _attn(proj4):
    _, b, s, _ = proj4.shape
    tile, blk, mid = ATT_TILE, ATT_SPAN, ATT_MID
    spec = lambda off: pl.BlockSpec((None, None, tile, LANES),
                                    lambda bi, pr, ti: ((BLK_B + off) * SLABS + pr, bi, ti, 0))
    return pl.pallas_call(
        _attn_kernel,
        grid=(b, SLABS, s // tile),
        in_specs=[spec(0), spec(1), spec(2)],
        out_specs=pl.BlockSpec((None, tile, LANES), lambda bi, pr, ti: (bi, ti, pr)),
        out_shape=jax.ShapeDtypeStruct((b, s, GROUP_W), F32),
        scratch_shapes=[pltpu.VMEM((tile, LANES), F32)] * 3
        + [pltpu.VMEM((blk + tile, LANES), BF16)] * 2
        + [pltpu.VMEM((mid * (blk + tile // mid), LANES), BF16)] * 2
        + [pltpu.VMEM((mid * mid * 2 * blk, LANES), BF16)] * 2
        + [pltpu.VMEM((len(ATT_DILATIONS), tile, LANES), F32)] * 2,
        compiler_params=_cparams("parallel", "parallel", "arbitrary"),
        name="dilattn",
    )(proj4, proj4, proj4)


def _causal_conv(x_ref, xpad_ref, w_ref, b_ref, first):
    c = x_ref.shape[1]

    @pl.when(first)
    def _():
        xpad_ref[0:SUBLANES, :] = jnp.zeros((SUBLANES, xpad_ref.shape[1]), F32)

    for p in range(x_ref.shape[0]):
        xpad_ref[SUBLANES:SUBLANES + c, p * LANES:(p + 1) * LANES] = x_ref[p]
    acc = b_ref[...] + w_ref[0:1, :] * xpad_ref[pl.ds(SUBLANES - CONV_W + 1, c), :]
    for j in range(1, CONV_W):
        acc = acc + w_ref[j:j + 1, :] * xpad_ref[pl.ds(SUBLANES - CONV_W + 1 + j, c), :]
    xpad_ref[0:SUBLANES, :] = xpad_ref[c:c + SUBLANES, :]
    return acc


def _pair_lanes(col_fn, first_head):
    return jnp.where(first_head, col_fn(0), col_fn(1))


def _ssd_kernel(z_ref, xbc_ref, dt_ref, cw_ref, cb_ref, dtb_ref, alog_ref, dsk_ref, nw_ref, tril_ref,
                y_ref, xpad_ref, st_ref):
    c = CHUNK
    first = pl.program_id(1) == 0

    @pl.when(first)
    def _():
        st_ref[...] = jnp.zeros_like(st_ref)

    xbc = _silu(_causal_conv(xbc_ref, xpad_ref, cw_ref, cb_ref, first))
    dt = _softplus(dt_ref[...] + dtb_ref[...])
    adt = dt * (-jnp.exp(alog_ref[...]))
    tril01 = tril_ref[...]
    acs = _sum01_matmul(tril01, adt)
    acs_t = acs.T
    e_cs = jnp.exp(acs)
    dt_rest = dt * jnp.exp(acs[c - 1:c, :] - acs)
    causal = tril01 > 0

    lane = lax.broadcasted_iota(jnp.int32, (1, LANES), 1)
    first_head = lane < SSM_HEAD_DIM
    gw = GROUP_W // SSM_GROUPS
    hpg = SSM_HEADS // SSM_GROUPS
    for g in range(SSM_GROUPS):
        bg = xbc[:, GROUP_W + g * SSM_STATE:GROUP_W + (g + 1) * SSM_STATE]
        cg = xbc[:, GROUP_W + (SSM_GROUPS + g) * SSM_STATE:GROUP_W + (SSM_GROUPS + g + 1) * SSM_STATE].astype(BF16)
        cb = _dot_nt(cg, bg.astype(BF16))
        st = st_ref[g]
        y_off = _dot(cg, st.astype(BF16))
        ys, xds, dec_last = [], [], []
        for pr in range(hpg // 2):
            h0 = g * hpg + 2 * pr
            ls = slice(h0 * SSM_HEAD_DIM, (h0 + 2) * SSM_HEAD_DIM)
            xs = xbc[:, ls]
            xdt = (xs * _pair_lanes(lambda i: dt[:, h0 + i:h0 + i + 1], first_head)).astype(BF16)
            yd = []
            for i in range(2):
                hh = h0 + i
                seg = acs[:, hh:hh + 1] - acs_t[hh:hh + 1, :]
                lmat = jnp.where(causal, jnp.exp(jnp.where(causal, seg, 0.0)), 0.0)
                yd.append(_dot((cb * lmat).astype(BF16), xdt))
            y_pair = (jnp.where(first_head, yd[0], yd[1])
                      + y_off[:, pr * LANES:(pr + 1) * LANES] * _pair_lanes(lambda i: e_cs[:, h0 + i:h0 + i + 1], first_head)
                      + dsk_ref[:, ls] * xs)
            ys.append(y_pair)
            xds.append((xs * _pair_lanes(lambda i: dt_rest[:, h0 + i:h0 + i + 1], first_head)).astype(BF16))
            dec_last.append(_pair_lanes(lambda i: e_cs[c - 1:c, h0 + i:h0 + i + 1], first_head))
        st_ref[g] = (jnp.concatenate(dec_last, axis=1) * st
                     + _dot_tn(bg.astype(BF16), jnp.concatenate(xds, axis=1)))
        zg = jnp.concatenate([z_ref[g * (gw // LANES) + p] for p in range(gw // LANES)], axis=1)
        yg = jnp.concatenate(ys, axis=1) * _silu(zg)
        y_ref[:, g * gw:(g + 1) * gw] = _rms(yg, nw_ref[:, g * gw:(g + 1) * gw]).astype(y_ref.dtype)


def _ssd(proj4, conv_w, conv_b, dt_bias, a_log, d_skip, norm_w):
    _, b, s, _ = proj4.shape
    full = lambda shape: pl.BlockSpec(shape, lambda bi, ci: (0,) * len(shape))
    tril = jnp.asarray(np.tril(np.ones((CHUNK, CHUNK), np.float32)), BF16)
    return pl.pallas_call(
        _ssd_kernel,
        grid=(b, s // CHUNK),
        in_specs=[
            _slab_spec(BLK_CZ, CHUNK),
            _slab_spec(BLK_CXBC, CHUNK, n=SSM_CONV_DIM // LANES),
            pl.BlockSpec((None, None, CHUNK, LANES), lambda bi, ci: (BLK_CDT * SLABS, bi, ci, 0)),
            full((CONV_W, SSM_CONV_DIM)), full((1, SSM_CONV_DIM)), full((1, LANES)), full((1, LANES)),
            full((1, GROUP_W)), full((1, GROUP_W)), full((CHUNK, CHUNK)),
        ],
        out_specs=pl.BlockSpec((None, CHUNK, GROUP_W), lambda bi, ci: (bi, ci, 0)),
        out_shape=jax.ShapeDtypeStruct((b, s, GROUP_W), BF16),
        scratch_shapes=[pltpu.VMEM((CHUNK + SUBLANES, SSM_CONV_DIM), F32),
                        pltpu.VMEM((SSM_GROUPS, SSM_STATE, GROUP_W // SSM_GROUPS), F32)],
        compiler_params=_cparams("parallel", "arbitrary"),
        name="ssd",
    )(proj4, proj4, proj4, conv_w, conv_b, dt_bias, a_log, d_skip, norm_w, tril)


def _gelu_tanh(x):
    return 0.5 * x * (1.0 + jnp.tanh(np.sqrt(2.0 / np.pi).astype(np.float32) * (x + 0.044715 * (x * x * x))))


def _lru_kernel(x_ref, g_ref, cw_ref, cb_ref, wa_ref, ba_ref, wx_ref, bx_ref, ap_ref, nw_ref,
                y_ref, xpad_ref, h_ref):
    c = CHUNK
    first = pl.program_id(1) == 0

    @pl.when(first)
    def _():
        h_ref[...] = jnp.zeros_like(h_ref)

    xc = _causal_conv(x_ref, xpad_ref, cw_ref, cb_ref, first)
    xb = xc.astype(BF16)
    r = jax.nn.sigmoid(_dot(xb, wa_ref[...]) + ba_ref[...])
    i = jax.nn.sigmoid(_dot(xb, wx_ref[...]) + bx_ref[...])
    log_a = -LRU_C * r * _softplus(-ap_ref[...])
    a = jnp.exp(log_a)
    bt = jnp.sqrt(jnp.maximum(-_expm1(2.0 * log_a), 0.0)) * (i * xc)
    row = lax.broadcasted_iota(jnp.int32, (c, 1), 0)
    d = 1
    while d < c:
        keep = row >= d
        a_s = pltpu.roll(a, d, 0)
        b_s = pltpu.roll(bt, d, 0)
        bt = jnp.where(keep, a * b_s + bt, bt)
        a = jnp.where(keep, a * a_s, a)
        d *= 2
    h = bt + a * h_ref[...]
    h_ref[...] = h[c - 1:c, :]
    gate = jnp.concatenate([g_ref[p] for p in range(SLABS)], axis=1)
    y_ref[...] = _rms(h * _gelu_tanh(gate), nw_ref[...]).astype(y_ref.dtype)


def _lru(proj4, conv_w, conv_b, wa_bd, b_a, wx_bd, b_x, a_param, norm_w):
    _, b, s, _ = proj4.shape
    full = lambda shape: pl.BlockSpec(shape, lambda bi, ci: (0,) * len(shape))
    vec = full((1, GROUP_W))
    return pl.pallas_call(
        _lru_kernel,
        grid=(b, s // CHUNK),
        in_specs=[
            _slab_spec(BLK_DX, CHUNK),
            _slab_spec(BLK_DG, CHUNK),
            full((CONV_W, GROUP_W)), vec, full((GROUP_W, GROUP_W)), vec, full((GROUP_W, GROUP_W)), vec, vec, vec,
        ],
        out_specs=pl.BlockSpec((None, CHUNK, GROUP_W), lambda bi, ci: (bi, ci, 0)),
        out_shape=jax.ShapeDtypeStruct((b, s, GROUP_W), BF16),
        scratch_shapes=[pltpu.VMEM((CHUNK + SUBLANES, GROUP_W), F32), pltpu.VMEM((1, GROUP_W), F32)],
        compiler_params=_cparams("parallel", "arbitrary"),
        name="rglru",
    )(proj4, proj4, conv_w, conv_b, wa_bd, b_a, wx_bd, b_x, a_param, norm_w)


def _block_diag(w):
    nb, n, _ = w.shape
    eye = jnp.eye(nb, dtype=w.dtype)
    return jnp.einsum("hij,hg->higj", w, eye).reshape(nb * n, nb * n)


def _pad_lanes(v, width=LANES):
    return jnp.pad(v, (0, width - v.shape[0])).reshape(1, width)


def kernel(x, ffn1_norm, ffn1_w_gate, ffn1_w_up, ffn1_w_down, mix_norm, w_in, w_out, hgrn_lb_logits, hgrn_norm, attn_norm, ssm_conv_w, ssm_conv_b, ssm_dt_bias, ssm_a_log, ssm_d, ssm_norm, lru_conv_w, lru_conv_b, lru_w_a, lru_b_a, lru_w_x, lru_b_x, lru_a_param, lru_norm, ffn2_norm, ffn2_w_gate, ffn2_w_up, ffn2_w_down, final_norm):
    bsz, seq, d = x.shape
    depth = w_in.shape[0]
    t = bsz * seq
    row = lambda v: v.reshape(1, -1).astype(F32)
    x2 = x.reshape(t, d)
    lb_logits = hgrn_lb_logits.astype(F32)
    ffn1_w = (_tile_cols(ffn1_w_gate, FFN_TF), _tile_cols(ffn1_w_up, FFN_TF), _tile_cols(ffn1_w_down, FFN_TN))
    ffn2_w = (_tile_cols(ffn2_w_gate, FFN_TF), _tile_cols(ffn2_w_up, FFN_TF), _tile_cols(ffn2_w_down, FFN_TN))
    w_in_t = _tile_cols(jnp.concatenate(
        [w_in[:, :, :N_IN_BEFORE_PAD],
         jnp.zeros((depth, d, (BLK_DX * GROUP_W) - N_IN_BEFORE_PAD), w_in.dtype),
         w_in[:, :, N_IN_BEFORE_PAD:]], axis=2), GROUP_W)
    for l in range(depth):
        x2 = _ffn(x2, row(ffn1_norm[l]), *ffn1_w, l)
        proj4 = _inproj(x2, row(mix_norm[l]), w_in_t, l).reshape(PROJ_BLOCKS * SLABS, bsz, seq, LANES)
        y_a = _hgrn(proj4, lb_logits, row(hgrn_norm[l]), layer=l).reshape(t, GROUP_W)
        y_b = _attn(proj4).reshape(t, GROUP_W)
        y_c = _ssd(proj4, ssm_conv_w[l].astype(F32), row(ssm_conv_b[l]), _pad_lanes(ssm_dt_bias[l].astype(F32)),
                   _pad_lanes(ssm_a_log[l].astype(F32)), row(jnp.repeat(ssm_d[l], SSM_HEAD_DIM)),
                   row(ssm_norm[l])).reshape(t, GROUP_W)
        y_d = _lru(proj4, lru_conv_w[l].astype(F32), row(lru_conv_b[l]), _block_diag(lru_w_a[l]).astype(BF16),
                   row(lru_b_a[l]), _block_diag(lru_w_x[l]).astype(BF16), row(lru_b_x[l]), row(lru_a_param[l]),
                   row(lru_norm[l])).reshape(t, GROUP_W)
        x2 = _outproj(x2, y_a, y_b, y_c, y_d, row(attn_norm[l]), w_out[l].astype(BF16))
        x2 = _ffn(x2, row(ffn2_norm[l]), *ffn2_w, l)
    return _final_norm(x2, row(final_norm)).reshape(bsz, seq, d)
```

```python
import functools

import numpy as np
import jax
import jax.numpy as jnp
from jax import lax
from jax.experimental import pallas as pl
from jax.experimental.pallas import tpu as pltpu

F32 = jnp.float32
BF16 = jnp.bfloat16

NORM_EPS = 1e-6
MASK_VALUE = -1e30
GROUP_W = 512
HG_HEADS = 4
HG_HEAD_W = GROUP_W // HG_HEADS
ATT_HEAD_DIM = 64
ATT_SPAN = 128
ATT_DILATIONS = (1, 4, 16)
SSM_HEADS = 8
SSM_HEAD_DIM = 64
SSM_GROUPS = 2
SSM_STATE = 128
SSM_CONV_DIM = GROUP_W + 2 * SSM_GROUPS * SSM_STATE
CONV_W = 4
LRU_C = 8.0

LANES = 128
SUBLANES = 8
VMEM_LIMIT = 56 * 1024 * 1024

PROJ_BLOCKS = 13
PROJ_W = PROJ_BLOCKS * GROUP_W
BLK_A = 0
BLK_B = 4
BLK_CZ = 7
BLK_CXBC = 8
BLK_CDT = 10
BLK_DX = 11
BLK_DG = 12
N_IN_BEFORE_PAD = 10 * GROUP_W + SSM_HEADS
SLABS = GROUP_W // LANES

CHUNK = 256
CHUNK_LEVELS = 8
STEP_CHUNKS = 2
STEP_ROWS = STEP_CHUNKS * CHUNK


def _cparams(*sem):
    return pltpu.CompilerParams(dimension_semantics=sem, vmem_limit_bytes=VMEM_LIMIT)


def _rms(x, w):
    return x * lax.rsqrt(jnp.mean(x * x, axis=-1, keepdims=True) + NORM_EPS) * w


def _silu(x):
    return x * jax.nn.sigmoid(x)


def _softplus(x):
    return jnp.maximum(x, 0.0) + jnp.log1p(jnp.exp(-jnp.abs(x)))


def _expm1(u):
    w = jnp.exp(u)
    near = (jnp.abs(u) < 0.5) & (w != 1.0)
    kahan = (w - 1.0) * u / jnp.log(jnp.where(near, w, 2.0))
    return jnp.where(near, kahan, jnp.where(w == 1.0, u, w - 1.0))


def _dot(a, b):
    return jnp.dot(a, b, preferred_element_type=F32)


def _dot_nt(a, b):
    return lax.dot_general(a, b, (((1,), (1,)), ((), ())), preferred_element_type=F32)


def _dot_tn(a, b):
    return lax.dot_general(a, b, (((0,), (0,)), ((), ())), preferred_element_type=F32)


def _sum01_matmul(m01, x):
    hi = x.astype(BF16)
    r1 = x - hi.astype(F32)
    mid = r1.astype(BF16)
    lo = (r1 - mid.astype(F32)).astype(BF16)
    return _dot(m01, hi) + _dot(m01, mid) + _dot(m01, lo)


def _sum01_matmul2(m01, x):
    hi = x.astype(BF16)
    mid = (x - hi.astype(F32)).astype(BF16)
    both = _dot(m01, jnp.concatenate([hi, mid], axis=1))
    n = x.shape[1]
    return both[:, :n] + both[:, n:]


def _ffn_kernel(x_ref, xc_ref, nw_ref, wg_ref, wu_ref, wd_ref, o_ref, h_ref, a_ref, *, n_f, tf):
    j = pl.program_id(1)

    @pl.when(j == 0)
    def _():
        h_ref[...] = _rms(x_ref[...], nw_ref[...]).astype(BF16)

    @pl.when(j < n_f)
    def _():
        h = h_ref[...]
        a_ref[j] = (_silu(_dot(h, wg_ref[...])) * _dot(h, wu_ref[...])).astype(BF16)

    @pl.when(j >= n_f)
    def _():
        acc = _dot(a_ref[0], wd_ref[0:tf, :])
        for jj in range(1, n_f):
            acc = acc + _dot(a_ref[jj], wd_ref[jj * tf:(jj + 1) * tf, :])
        o_ref[...] = xc_ref[...] + 0.5 * acc


FFN_TM = 1024
FFN_TF = 512
FFN_TN = 256


def _tile_cols(w, tile):
    nl, k, n = w.shape
    return w.astype(BF16).reshape(nl, k, n // tile, tile).transpose(0, 2, 1, 3)


def _ffn(x2, nw, wg_t, wu_t, wd_t, layer, *, tm=FFN_TM):
    t, d = x2.shape
    _, n_f, _, tf = wg_t.shape
    _, n_d, f, tn = wd_t.shape
    up = lambda i, j: (layer, jnp.minimum(j, n_f - 1), 0, 0)
    down = lambda i, j: (layer, jnp.maximum(j - n_f, 0), 0, 0)
    col = lambda i, j: (i, jnp.maximum(j - n_f, 0))
    return pl.pallas_call(
        functools.partial(_ffn_kernel, n_f=n_f, tf=tf),
        grid=(t // tm, n_f + n_d),
        in_specs=[
            pl.BlockSpec((tm, d), lambda i, j: (i, 0)),
            pl.BlockSpec((tm, tn), col),
            pl.BlockSpec((1, d), lambda i, j: (0, 0)),
            pl.BlockSpec((None, None, d, tf), up),
            pl.BlockSpec((None, None, d, tf), up),
            pl.BlockSpec((None, None, f, tn), down),
        ],
        out_specs=pl.BlockSpec((tm, tn), col),
        out_shape=jax.ShapeDtypeStruct((t, d), F32),
        scratch_shapes=[pltpu.VMEM((tm, d), BF16), pltpu.VMEM((n_f, tm, tf), BF16)],
        compiler_params=_cparams("parallel", "arbitrary"),
        name="ffn",
    )(x2, x2, nw, wg_t, wu_t, wd_t)


def _final_norm_kernel(x_ref, w_ref, o_ref):
    o_ref[...] = _rms(x_ref[...], w_ref[...])


def _final_norm(x2, w, *, tm=1024):
    t, d = x2.shape
    return pl.pallas_call(
        _final_norm_kernel,
        grid=(t // tm,),
        in_specs=[pl.BlockSpec((tm, d), lambda i: (i, 0)), pl.BlockSpec((1, d), lambda i: (0, 0))],
        out_specs=pl.BlockSpec((tm, d), lambda i: (i, 0)),
        out_shape=jax.ShapeDtypeStruct((t, d), F32),
        compiler_params=_cparams("parallel"),
        name="final_norm",
    )(x2, w)


def _inproj_kernel(x_ref, nw_ref, w_ref, o_ref, h_ref):
    @pl.when(pl.program_id(1) == 0)
    def _():
        h_ref[...] = _rms(x_ref[...], nw_ref[...]).astype(BF16)

    res = _dot(h_ref[...], w_ref[...])
    for p in range(o_ref.shape[0]):
        o_ref[p] = res[:, p * LANES:(p + 1) * LANES]


INPROJ_TN = PROJ_W // 4


def _inproj(x2, nw, w_t, layer, *, tm=1024):
    t, d = x2.shape
    _, n_t, _, tn = w_t.shape
    return pl.pallas_call(
        _inproj_kernel,
        grid=(t // tm, n_t),
        in_specs=[
            pl.BlockSpec((tm, d), lambda i, j: (i, 0)),
            pl.BlockSpec((1, d), lambda i, j: (0, 0)),
            pl.BlockSpec((None, None, d, tn), lambda i, j: (layer, j, 0, 0)),
        ],
        out_specs=pl.BlockSpec((tn // LANES, tm, LANES), lambda i, j: (j, i, 0)),
        out_shape=jax.ShapeDtypeStruct((PROJ_BLOCKS * SLABS, t, LANES), F32),
        scratch_shapes=[pltpu.VMEM((tm, d), BF16)],
        compiler_params=_cparams("parallel", "arbitrary"),
        name="inproj",
    )(x2, nw, w_t)


def _outproj_kernel(x_ref, ya_ref, yb_ref, yc_ref, yd_ref, bnw_ref, w_ref, o_ref):
    yb = _rms(yb_ref[...], bnw_ref[...]).astype(BF16)
    acc = x_ref[...]
    for g, y in enumerate((ya_ref[...], yb, yc_ref[...], yd_ref[...])):
        acc = acc + _dot(y, w_ref[g * GROUP_W:(g + 1) * GROUP_W, :])
    o_ref[...] = acc


def _outproj(x2, ya, yb, yc, yd, attn_norm_w, w, *, tm=512):
    t, d = x2.shape
    yspec = pl.BlockSpec((tm, GROUP_W), lambda i: (i, 0))
    return pl.pallas_call(
        _outproj_kernel,
        grid=(t // tm,),
        in_specs=[pl.BlockSpec((tm, d), lambda i: (i, 0)), yspec, yspec, yspec, yspec,
                  pl.BlockSpec((1, GROUP_W), lambda i: (0, 0)), pl.BlockSpec((4 * GROUP_W, d), lambda i: (0, 0))],
        out_specs=pl.BlockSpec((tm, d), lambda i: (i, 0)),
        out_shape=jax.ShapeDtypeStruct((t, d), F32),
        compiler_params=_cparams("parallel"),
        name="outproj",
    )(x2, ya, yb, yc, yd, attn_norm_w, w)


def _hgrn_tables(c=CHUNK, levels=CHUNK_LEVELS):
    t = np.arange(c)
    rows = []
    for lvl in range(levels):
        h = 1 << lvl
        r = (t // (2 * h)) * (2 * h) + h
        up = (t // h) % 2 == 1
        u = t[None, :]
        m = np.where(up[:, None], (u > r[:, None]) & (u <= t[:, None]), (u > t[:, None]) & (u <= r[:, None]))
        rows.append(m)
    rows.append(t[None, :] <= t[:, None])
    rows.append(t[None, :] > t[:, None])
    mstack = np.concatenate(rows, axis=0).astype(np.float32)
    x = t[:, None] ^ t[None, :]
    lv = np.where(t[None, :] < t[:, None], np.floor(np.log2(np.maximum(x, 1))).astype(np.int32), -1)
    return mstack, lv.astype(np.int32)


_HGRN_MSTACK, _HGRN_LEVEL = _hgrn_tables()


def _hgrn_kernel(q_ref, f_ref, i_ref, g_ref, lbl_ref, nw_ref, mst_ref, lv_ref, o_ref, st_ref, *, layer):
    c = CHUNK

    @pl.when(pl.program_id(1) == 0)
    def _():
        st_ref[...] = jnp.zeros_like(st_ref)

    lbl = lbl_ref[...]
    e = jnp.exp(lbl - jnp.max(lbl, axis=0, keepdims=True))
    p = e / jnp.sum(e, axis=0, keepdims=True)
    lb = jnp.zeros((1, GROUP_W), F32)
    for i in range(1, layer + 1):
        lb = lb + p[i:i + 1, :]

    lax.fori_loop(0, STEP_CHUNKS, functools.partial(
        _hgrn_chunk, q_ref, f_ref, i_ref, g_ref, nw_ref, mst_ref, lv_ref, o_ref, st_ref, lb), 0)


def _hgrn_chunk(q_ref, f_ref, i_ref, g_ref, nw_ref, mst_ref, lv_ref, o_ref, st_ref, lb, ci, carry):
    c = CHUNK
    rows = pl.ds(pl.multiple_of(ci * c, c), c)
    mst = mst_ref[...]
    lv = lv_ref[...]
    row = lax.broadcasted_iota(jnp.int32, (c, 1), 0)
    for h in range(HG_HEADS):
        sl = slice(h * HG_HEAD_W, (h + 1) * HG_HEAD_W)
        q = _silu(q_ref[h, rows, :])
        z = f_ref[h, rows, :]
        v = i_ref[h, rows, :]
        lbh = lb[:, sl]
        g = jnp.log(lbh + (1.0 - lbh) * jax.nn.sigmoid(z))
        k = (1.0 - lbh) * jax.nn.sigmoid(-z)
        ex = jnp.exp(_sum01_matmul2(mst, g))
        a = jnp.zeros((c, c), F32)
        for lvl in range(CHUNK_LEVELS):
            exl = ex[lvl * c:(lvl + 1) * c]
            up = ((row >> lvl) & 1) == 1
            qp = jnp.where(up, q * exl, 0.0).astype(BF16)
            kp = jnp.where(up, 0.0, k * exl).astype(BF16)
            a = jnp.where(lv == lvl, _dot_nt(qp, kp), a)
        vb = v.astype(BF16)
        ex_b = ex[CHUNK_LEVELS * c:(CHUNK_LEVELS + 1) * c]
        ex_r = ex[(CHUNK_LEVELS + 1) * c:(CHUNK_LEVELS + 2) * c]
        st = st_ref[h]
        o = (_dot(a.astype(BF16), vb) + jnp.sum(q * k, axis=-1, keepdims=True) * v
             + _dot_nt((q * ex_b).astype(BF16), st.astype(BF16)))
        st_ref[h] = ex_b[c - 1:c, :] * st + _dot_tn(vb, (k * ex_r).astype(BF16))
        o = _rms(o, nw_ref[:, sl]) * _silu(g_ref[h, rows, :])
        o_ref[rows, sl] = o.astype(o_ref.dtype)
    return carry


def _slab_spec(blk, rows, n=SLABS):
    first = blk * SLABS // n
    return pl.BlockSpec((n, None, rows, LANES), lambda bi, ci: (first, bi, ci, 0))


def _hgrn(proj4, lb_logits, norm_w, *, layer):
    _, b, s, _ = proj4.shape
    cspec = lambda blk: _slab_spec(blk, STEP_ROWS)
    full = lambda shape: pl.BlockSpec(shape, lambda bi, ci: (0,) * len(shape))
    mst = jnp.asarray(_HGRN_MSTACK, BF16)
    lv = jnp.asarray(_HGRN_LEVEL)
    return pl.pallas_call(
        functools.partial(_hgrn_kernel, layer=layer),
        grid=(b, s // STEP_ROWS),
        in_specs=[cspec(BLK_A), cspec(BLK_A + 1), cspec(BLK_A + 2), cspec(BLK_A + 3),
                  full(lb_logits.shape), full((1, GROUP_W)), full(mst.shape), full(lv.shape)],
        out_specs=pl.BlockSpec((None, STEP_ROWS, GROUP_W), lambda bi, ci: (bi, ci, 0)),
        out_shape=jax.ShapeDtypeStruct((b, s, GROUP_W), BF16),
        scratch_shapes=[pltpu.VMEM((HG_HEADS, HG_HEAD_W, HG_HEAD_W), F32)],
        compiler_params=_cparams("parallel", "arbitrary"),
        name="hgrn2",
    )(proj4, proj4, proj4, proj4, lb_logits, norm_w, mst, lv)


ATT_TILE = 2048
ATT_GROUP = 4
ATT_MID = 4
assert ATT_DILATIONS == (1, ATT_MID, ATT_MID * ATT_MID) and ATT_TILE == ATT_SPAN * ATT_DILATIONS[-1]


def _attend_group(qs, ks, vs, first_flags, bias_all, bias_first, first_head):
    blk = ATT_SPAN
    scores = []
    for q2, k2, ff in zip(qs, ks, first_flags):
        qq = jnp.concatenate([jnp.where(first_head, q2, 0.0), jnp.where(first_head, 0.0, q2)], axis=0).astype(BF16)
        bias = jnp.where(ff > 0, bias_first, bias_all)
        scores.append(_dot_nt(qq, k2) + jnp.concatenate([bias, bias], axis=0))
    s = jnp.concatenate(scores, axis=0)
    m = jnp.max(s, axis=-1, keepdims=True)
    p = jnp.exp(s - m)
    den = jnp.sum(p, axis=-1, keepdims=True)
    lse = m + jnp.log(den)
    pb = p.astype(BF16)
    outs = []
    for g, v2 in enumerate(vs):
        r0 = g * 2 * blk
        pv = _dot(pb[r0:r0 + 2 * blk], v2)
        d0, d1 = den[r0:r0 + blk], den[r0 + blk:r0 + 2 * blk]
        l0, l1 = lse[r0:r0 + blk], lse[r0 + blk:r0 + 2 * blk]
        o2 = jnp.where(first_head, pv[:blk], pv[blk:]) / jnp.where(first_head, d0, d1)
        outs.append((o2, jnp.where(first_head, l0, l1)))
    return outs


def _attn_kernel(q_ref, k_ref, v_ref, y_ref, q4_ref, k4_ref, v4_ref, kd1, vd1, kd4, vd4, kd16, vd16, op_ref, lp_ref):
    i = pl.program_id(2)
    blk, tile, mid = ATT_SPAN, ATT_TILE, ATT_MID
    len4 = tile // mid
    pitch4 = blk + len4
    pitch16 = 2 * blk
    scale = ATT_HEAD_DIM ** -0.5
    zeros = jnp.zeros((blk, LANES), BF16)

    @pl.when(i == 0)
    def _():
        for kd, vd, pitch, nres in ((kd1, vd1, 0, 1), (kd4, vd4, pitch4, mid), (kd16, vd16, pitch16, mid * mid)):
            for r in range(nres):
                kd[r * pitch:r * pitch + blk, :] = zeros
                vd[r * pitch:r * pitch + blk, :] = zeros

    kd1[blk:blk + tile, :] = k_ref[...].astype(BF16)
    vd1[blk:blk + tile, :] = v_ref[...].astype(BF16)
    for r in range(mid):
        rows = slice(r * len4, (r + 1) * len4)
        cur = slice(r * pitch4 + blk, (r + 1) * pitch4)
        kk = k_ref[pl.ds(r, len4, stride=mid), :]
        vv = v_ref[pl.ds(r, len4, stride=mid), :]
        k4_ref[rows, :] = kk
        v4_ref[rows, :] = vv
        kd4[cur, :] = kk.astype(BF16)
        vd4[cur, :] = vv.astype(BF16)
        q4_ref[rows, :] = q_ref[pl.ds(r, len4, stride=mid), :] * scale
    for r4 in range(mid):
        for hi in range(mid):
            r16 = r4 + mid * hi
            cur = slice(r16 * pitch16 + blk, (r16 + 1) * pitch16)
            kd16[cur, :] = k4_ref[pl.ds(r4 * len4 + hi, blk, stride=mid), :].astype(BF16)
            vd16[cur, :] = v4_ref[pl.ds(r4 * len4 + hi, blk, stride=mid), :].astype(BF16)

    iq = lax.broadcasted_iota(jnp.int32, (blk, 2 * blk), 0)
    ik = lax.broadcasted_iota(jnp.int32, (blk, 2 * blk), 1)
    dist = iq + blk - ik
    valid = (dist >= 0) & (dist <= blk)
    bias_all = jnp.where(valid, 0.0, MASK_VALUE)
    bias_first = jnp.where(valid & (ik >= blk), 0.0, MASK_VALUE)
    lane = lax.broadcasted_iota(jnp.int32, (1, LANES), 1)
    first_head = lane < ATT_HEAD_DIM
    tile0 = 1 - jnp.minimum(i, 1)
    grp = ATT_GROUP

    def attend(qs, kd, vd, bases, flags):
        ks = [kd[pl.ds(pl.multiple_of(b0, blk), 2 * blk), :] for b0 in bases]
        vs = [vd[pl.ds(pl.multiple_of(b0, blk), 2 * blk), :] for b0 in bases]
        return _attend_group(qs, ks, vs, flags, bias_all, bias_first, first_head)

    def pattern1(it, carry):
        b0s = [(it * grp + g) * blk for g in range(grp)]
        qs = [q_ref[pl.ds(pl.multiple_of(b0, blk), blk), :] * scale for b0 in b0s]
        flags = [tile0 * (1 - jnp.minimum(it * grp + g, 1)) for g in range(grp)]
        for b0, (o2, l2) in zip(b0s, attend(qs, kd1, vd1, b0s, flags)):
            op_ref[0, pl.ds(pl.multiple_of(b0, blk), blk), :] = o2
            lp_ref[0, pl.ds(pl.multiple_of(b0, blk), blk), :] = l2
        return carry

    def pattern4(r, carry):
        qs = [q4_ref[pl.ds(pl.multiple_of(r * len4 + g * blk, blk), blk), :] for g in range(grp)]
        bases = [r * pitch4 + g * blk for g in range(grp)]
        flags = [tile0 if g == 0 else 0 * tile0 for g in range(grp)]
        for g, (o2, l2) in enumerate(attend(qs, kd4, vd4, bases, flags)):
            op_ref[1, pl.ds(r + mid * blk * g, blk, stride=mid), :] = o2
            lp_ref[1, pl.ds(r + mid * blk * g, blk, stride=mid), :] = l2
        return carry

    def pattern16(hi, carry):
        qs = [q4_ref[pl.ds(g * len4 + hi, blk, stride=mid), :] for g in range(grp)]
        bases = [(g + mid * hi) * pitch16 for g in range(grp)]
        flags = [tile0] * grp
        for g, (o2, l2) in enumerate(attend(qs, kd16, vd16, bases, flags)):
            op_ref[2, pl.ds(g + mid * hi, blk, stride=mid * mid), :] = o2
            lp_ref[2, pl.ds(g + mid * hi, blk, stride=mid * mid), :] = l2
        return carry

    assert grp == mid and tile // blk == grp * mid
    lax.fori_loop(0, mid, pattern1, 0)
    lax.fori_loop(0, mid, pattern4, 0)
    lax.fori_loop(0, mid, pattern16, 0)

    for kd, vd, pitch, nres, ln in ((kd1, vd1, 0, 1, tile), (kd4, vd4, pitch4, mid, len4),
                                    (kd16, vd16, pitch16, mid * mid, blk)):
        for r in range(nres):
            kd[r * pitch:r * pitch + blk, :] = kd[r * pitch + ln:r * pitch + ln + blk, :]
            vd[r * pitch:r * pitch + blk, :] = vd[r * pitch + ln:r * pitch + ln + blk, :]

    def merge(c, carry):
        rows = pl.ds(pl.multiple_of(c * CHUNK, CHUNK), CHUNK)
        la, lb, lc = lp_ref[0, rows, :], lp_ref[1, rows, :], lp_ref[2, rows, :]
        m = jnp.maximum(jnp.maximum(la, lb), lc)
        wa, wb, wc = jnp.exp(la - m), jnp.exp(lb - m), jnp.exp(lc - m)
        y_ref[rows, :] = (wa * op_ref[0, rows, :] + wb * op_ref[1, rows, :] + wc * op_ref[2, rows, :]) / (wa + wb + wc)
        return carry

    lax.fori_loop(0, tile // CHUNK, merge, 0)


def _attn(proj4):
    _, b, s, _ = proj4.shape
    tile, blk, mid = ATT_TILE, ATT_SPAN, ATT_MID
    spec = lambda off: pl.BlockSpec((None, None, tile, LANES),
                                    lambda bi, pr, ti: ((BLK_B + off) * SLABS + pr, bi, ti, 0))
    return pl.pallas_call(
        _attn_kernel,
        grid=(b, SLABS, s // tile),
        in_specs=[spec(0), spec(1), spec(2)],
        out_specs=pl.BlockSpec((None, tile, LANES), lambda bi, pr, ti: (bi, ti, pr)),
        out_shape=jax.ShapeDtypeStruct((b, s, GROUP_W), F32),
        scratch_shapes=[pltpu.VMEM((tile, LANES), F32)] * 3
        + [pltpu.VMEM((blk + tile, LANES), BF16)] * 2
        + [pltpu.VMEM((mid * (blk + tile // mid), LANES), BF16)] * 2
        + [pltpu.VMEM((mid * mid * 2 * blk, LANES), BF16)] * 2
        + [pltpu.VMEM((len(ATT_DILATIONS), tile, LANES), F32)] * 2,
        compiler_params=_cparams("parallel", "parallel", "arbitrary"),
        name="dilattn",
    )(proj4, proj4, proj4)


def _causal_conv(x_ref, rows, xpad_ref, w_ref, b_ref):
    c = CHUNK
    for p in range(x_ref.shape[0]):
        xpad_ref[SUBLANES:SUBLANES + c, p * LANES:(p + 1) * LANES] = x_ref[p, rows, :]
    acc = b_ref[...] + w_ref[0:1, :] * xpad_ref[pl.ds(SUBLANES - CONV_W + 1, c), :]
    for j in range(1, CONV_W):
        acc = acc + w_ref[j:j + 1, :] * xpad_ref[pl.ds(SUBLANES - CONV_W + 1 + j, c), :]
    xpad_ref[0:SUBLANES, :] = xpad_ref[c:c + SUBLANES, :]
    return acc


def _pair_lanes(col_fn, first_head):
    return jnp.where(first_head, col_fn(0), col_fn(1))


def _ssd_kernel(z_ref, xbc_ref, dt_ref, cw_ref, cb_ref, dtb_ref, alog_ref, dsk_ref, nw_ref, tril_ref,
                y_ref, xpad_ref, st_ref):
    @pl.when(pl.program_id(1) == 0)
    def _():
        st_ref[...] = jnp.zeros_like(st_ref)
        xpad_ref[0:SUBLANES, :] = jnp.zeros((SUBLANES, xpad_ref.shape[1]), F32)

    lax.fori_loop(0, STEP_CHUNKS, functools.partial(
        _ssd_chunk, z_ref, xbc_ref, dt_ref, cw_ref, cb_ref, dtb_ref, alog_ref, dsk_ref, nw_ref, tril_ref,
        y_ref, xpad_ref, st_ref), 0)


def _ssd_chunk(z_ref, xbc_ref, dt_ref, cw_ref, cb_ref, dtb_ref, alog_ref, dsk_ref, nw_ref, tril_ref,
               y_ref, xpad_ref, st_ref, ci, carry):
    c = CHUNK
    rows = pl.ds(pl.multiple_of(ci * c, c), c)
    xbc = _silu(_causal_conv(xbc_ref, rows, xpad_ref, cw_ref, cb_ref))
    dt = _softplus(dt_ref[rows, :] + dtb_ref[...])
    adt = dt * (-jnp.exp(alog_ref[...]))
    tril01 = tril_ref[...]
    acs = _sum01_matmul(tril01, adt)
    acs_t = acs.T
    e_cs = jnp.exp(acs)
    dt_rest = dt * jnp.exp(acs[c - 1:c, :] - acs)
    causal = tril01 > 0

    lane = lax.broadcasted_iota(jnp.int32, (1, LANES), 1)
    first_head = lane < SSM_HEAD_DIM
    gw = GROUP_W // SSM_GROUPS
    hpg = SSM_HEADS // SSM_GROUPS
    for g in range(SSM_GROUPS):
        bg = xbc[:, GROUP_W + g * SSM_STATE:GROUP_W + (g + 1) * SSM_STATE]
        cg = xbc[:, GROUP_W + (SSM_GROUPS + g) * SSM_STATE:GROUP_W + (SSM_GROUPS + g + 1) * SSM_STATE].astype(BF16)
        cb = jnp.where(causal, _dot_nt(cg, bg.astype(BF16)), 0.0)
        st = st_ref[g]
        y_off = _dot(cg, st.astype(BF16))
        ys, xds, dec_last = [], [], []
        for pr in range(hpg // 2):
            h0 = g * hpg + 2 * pr
            ls = slice(h0 * SSM_HEAD_DIM, (h0 + 2) * SSM_HEAD_DIM)
            xs = xbc[:, ls]
            xdt = (xs * _pair_lanes(lambda i: dt[:, h0 + i:h0 + i + 1], first_head)).astype(BF16)
            yd = []
            for i in range(2):
                hh = h0 + i
                seg = jnp.minimum(acs[:, hh:hh + 1] - acs_t[hh:hh + 1, :], 0.0)
                yd.append(_dot((cb * jnp.exp(seg)).astype(BF16), xdt))
            y_pair = (jnp.where(first_head, yd[0], yd[1])
                      + y_off[:, pr * LANES:(pr + 1) * LANES] * _pair_lanes(lambda i: e_cs[:, h0 + i:h0 + i + 1], first_head)
                      + dsk_ref[:, ls] * xs)
            ys.append(y_pair)
            xds.append((xs * _pair_lanes(lambda i: dt_rest[:, h0 + i:h0 + i + 1], first_head)).astype(BF16))
            dec_last.append(_pair_lanes(lambda i: e_cs[c - 1:c, h0 + i:h0 + i + 1], first_head))
        st_ref[g] = (jnp.concatenate(dec_last, axis=1) * st
                     + _dot_tn(bg.astype(BF16), jnp.concatenate(xds, axis=1)))
        zg = jnp.concatenate([z_ref[g * (gw // LANES) + p, rows, :] for p in range(gw // LANES)], axis=1)
        yg = jnp.concatenate(ys, axis=1) * _silu(zg)
        y_ref[rows, g * gw:(g + 1) * gw] = _rms(yg, nw_ref[:, g * gw:(g + 1) * gw]).astype(y_ref.dtype)
    return carry


def _ssd(proj4, conv_w, conv_b, dt_bias, a_log, d_skip, norm_w):
    _, b, s, _ = proj4.shape
    full = lambda shape: pl.BlockSpec(shape, lambda bi, ci: (0,) * len(shape))
    tril = jnp.asarray(np.tril(np.ones((CHUNK, CHUNK), np.float32)), BF16)
    return pl.pallas_call(
        _ssd_kernel,
        grid=(b, s // STEP_ROWS),
        in_specs=[
            _slab_spec(BLK_CZ, STEP_ROWS),
            _slab_spec(BLK_CXBC, STEP_ROWS, n=SSM_CONV_DIM // LANES),
            pl.BlockSpec((None, None, STEP_ROWS, LANES), lambda bi, ci: (BLK_CDT * SLABS, bi, ci, 0)),
            full((CONV_W, SSM_CONV_DIM)), full((1, SSM_CONV_DIM)), full((1, LANES)), full((1, LANES)),
            full((1, GROUP_W)), full((1, GROUP_W)), full((CHUNK, CHUNK)),
        ],
        out_specs=pl.BlockSpec((None, STEP_ROWS, GROUP_W), lambda bi, ci: (bi, ci, 0)),
        out_shape=jax.ShapeDtypeStruct((b, s, GROUP_W), BF16),
        scratch_shapes=[pltpu.VMEM((CHUNK + SUBLANES, SSM_CONV_DIM), F32),
                        pltpu.VMEM((SSM_GROUPS, SSM_STATE, GROUP_W // SSM_GROUPS), F32)],
        compiler_params=_cparams("parallel", "arbitrary"),
        name="ssd",
    )(proj4, proj4, proj4, conv_w, conv_b, dt_bias, a_log, d_skip, norm_w, tril)


def _gelu_tanh(x):
    return 0.5 * x * (1.0 + jnp.tanh(np.sqrt(2.0 / np.pi).astype(np.float32) * (x + 0.044715 * (x * x * x))))


def _lru_kernel(x_ref, g_ref, cw_ref, cb_ref, wa_ref, ba_ref, wx_ref, bx_ref, ap_ref, nw_ref,
                y_ref, xpad_ref, h_ref):
    @pl.when(pl.program_id(1) == 0)
    def _():
        h_ref[...] = jnp.zeros_like(h_ref)
        xpad_ref[0:SUBLANES, :] = jnp.zeros((SUBLANES, xpad_ref.shape[1]), F32)

    lax.fori_loop(0, STEP_CHUNKS, functools.partial(
        _lru_chunk, x_ref, g_ref, cw_ref, cb_ref, wa_ref, ba_ref, wx_ref, bx_ref, ap_ref, nw_ref,
        y_ref, xpad_ref, h_ref), 0)


def _lru_chunk(x_ref, g_ref, cw_ref, cb_ref, wa_ref, ba_ref, wx_ref, bx_ref, ap_ref, nw_ref,
               y_ref, xpad_ref, h_ref, ci, carry):
    c = CHUNK
    rows = pl.ds(pl.multiple_of(ci * c, c), c)
    xc = _causal_conv(x_ref, rows, xpad_ref, cw_ref, cb_ref)
    xb = xc.astype(BF16)
    r = jax.nn.sigmoid(_dot(xb, wa_ref[...]) + ba_ref[...])
    i = jax.nn.sigmoid(_dot(xb, wx_ref[...]) + bx_ref[...])
    log_a = -LRU_C * r * _softplus(-ap_ref[...])
    a = jnp.exp(log_a)
    bt = jnp.sqrt(jnp.maximum(-_expm1(2.0 * log_a), 0.0)) * (i * xc)
    row = lax.broadcasted_iota(jnp.int32, (c, 1), 0)
    d = 1
    while d < c:
        keep = row >= d
        a_s = pltpu.roll(a, d, 0)
        b_s = pltpu.roll(bt, d, 0)
        bt = jnp.where(keep, a * b_s + bt, bt)
        a = jnp.where(keep, a * a_s, a)
        d *= 2
    h = bt + a * h_ref[...]
    h_ref[...] = h[c - 1:c, :]
    gate = jnp.concatenate([g_ref[p, rows, :] for p in range(SLABS)], axis=1)
    y_ref[rows, :] = _rms(h * _gelu_tanh(gate), nw_ref[...]).astype(y_ref.dtype)
    return carry


def _lru(proj4, conv_w, conv_b, wa_bd, b_a, wx_bd, b_x, a_param, norm_w):
    _, b, s, _ = proj4.shape
    full = lambda shape: pl.BlockSpec(shape, lambda bi, ci: (0,) * len(shape))
    vec = full((1, GROUP_W))
    return pl.pallas_call(
        _lru_kernel,
        grid=(b, s // STEP_ROWS),
        in_specs=[
            _slab_spec(BLK_DX, STEP_ROWS),
            _slab_spec(BLK_DG, STEP_ROWS),
            full((CONV_W, GROUP_W)), vec, full((GROUP_W, GROUP_W)), vec, full((GROUP_W, GROUP_W)), vec, vec, vec,
        ],
        out_specs=pl.BlockSpec((None, STEP_ROWS, GROUP_W), lambda bi, ci: (bi, ci, 0)),
        out_shape=jax.ShapeDtypeStruct((b, s, GROUP_W), BF16),
        scratch_shapes=[pltpu.VMEM((CHUNK + SUBLANES, GROUP_W), F32), pltpu.VMEM((1, GROUP_W), F32)],
        compiler_params=_cparams("parallel", "arbitrary"),
        name="rglru",
    )(proj4, proj4, conv_w, conv_b, wa_bd, b_a, wx_bd, b_x, a_param, norm_w)


def _block_diag(w):
    nb, n, _ = w.shape
    eye = jnp.eye(nb, dtype=w.dtype)
    return jnp.einsum("hij,hg->higj", w, eye).reshape(nb * n, nb * n)


def _pad_lanes(v, width=LANES):
    return jnp.pad(v, (0, width - v.shape[0])).reshape(1, width)


def kernel(x, ffn1_norm, ffn1_w_gate, ffn1_w_up, ffn1_w_down, mix_norm, w_in, w_out, hgrn_lb_logits, hgrn_norm, attn_norm, ssm_conv_w, ssm_conv_b, ssm_dt_bias, ssm_a_log, ssm_d, ssm_norm, lru_conv_w, lru_conv_b, lru_w_a, lru_b_a, lru_w_x, lru_b_x, lru_a_param, lru_norm, ffn2_norm, ffn2_w_gate, ffn2_w_up, ffn2_w_down, final_norm):
    bsz, seq, d = x.shape
    depth = w_in.shape[0]
    t = bsz * seq
    row = lambda v: v.reshape(1, -1).astype(F32)
    x2 = x.reshape(t, d)
    lb_logits = hgrn_lb_logits.astype(F32)
    ffn1_w = (_tile_cols(ffn1_w_gate, FFN_TF), _tile_cols(ffn1_w_up, FFN_TF), _tile_cols(ffn1_w_down, FFN_TN))
    ffn2_w = (_tile_cols(ffn2_w_gate, FFN_TF), _tile_cols(ffn2_w_up, FFN_TF), _tile_cols(ffn2_w_down, FFN_TN))
    w_in_t = _tile_cols(jnp.concatenate(
        [w_in[:, :, :N_IN_BEFORE_PAD],
         jnp.zeros((depth, d, (BLK_DX * GROUP_W) - N_IN_BEFORE_PAD), w_in.dtype),
         w_in[:, :, N_IN_BEFORE_PAD:]], axis=2), INPROJ_TN)
    for l in range(depth):
        x2 = _ffn(x2, row(ffn1_norm[l]), *ffn1_w, l)
        proj4 = _inproj(x2, row(mix_norm[l]), w_in_t, l).reshape(PROJ_BLOCKS * SLABS, bsz, seq, LANES)
        y_a = _hgrn(proj4, lb_logits, row(hgrn_norm[l]), layer=l).reshape(t, GROUP_W)
        y_b = _attn(proj4).reshape(t, GROUP_W)
        y_c = _ssd(proj4, ssm_conv_w[l].astype(F32), row(ssm_conv_b[l]), _pad_lanes(ssm_dt_bias[l].astype(F32)),
                   _pad_lanes(ssm_a_log[l].astype(F32)), row(jnp.repeat(ssm_d[l], SSM_HEAD_DIM)),
                   row(ssm_norm[l])).reshape(t, GROUP_W)
        y_d = _lru(proj4, lru_conv_w[l].astype(F32), row(lru_conv_b[l]), _block_diag(lru_w_a[l]).astype(BF16),
                   row(lru_b_a[l]), _block_diag(lru_w_x[l]).astype(BF16), row(lru_b_x[l]), row(lru_a_param[l]),
                   row(lru_norm[l])).reshape(t, GROUP_W)
        x2 = _outproj(x2, y_a, y_b, y_c, y_d, row(attn_norm[l]), w_out[l].astype(BF16))
        x2 = _ffn(x2, row(ffn2_norm[l]), *ffn2_w, l)
    return _final_norm(x2, row(final_norm)).reshape(bsz, seq, d)
```

```python
import functools

import numpy as np
import jax
import jax.numpy as jnp
from jax import lax
from jax.experimental import pallas as pl
from jax.experimental.pallas import tpu as pltpu

F32 = jnp.float32
BF16 = jnp.bfloat16

NORM_EPS = 1e-6
MASK_VALUE = -1e30
GROUP_W = 512
HG_HEADS = 4
HG_HEAD_W = GROUP_W // HG_HEADS
ATT_HEAD_DIM = 64
ATT_SPAN = 128
ATT_DILATIONS = (1, 4, 16)
SSM_HEADS = 8
SSM_HEAD_DIM = 64
SSM_GROUPS = 2
SSM_STATE = 128
SSM_CONV_DIM = GROUP_W + 2 * SSM_GROUPS * SSM_STATE
CONV_W = 4
LRU_C = 8.0

LANES = 128
SUBLANES = 8
VMEM_LIMIT = 60 * 1024 * 1024

PROJ_BLOCKS = 13
PROJ_W = PROJ_BLOCKS * GROUP_W
BLK_A = 0
BLK_B = 4
BLK_CZ = 7
BLK_CXBC = 8
BLK_CDT = 10
BLK_DX = 11
BLK_DG = 12
N_IN_BEFORE_PAD = 10 * GROUP_W + SSM_HEADS
SLABS = GROUP_W // LANES

CHUNK = 256
CHUNK_LEVELS = 8
STEP_CHUNKS = 2
STEP_ROWS = STEP_CHUNKS * CHUNK


def _cparams(*sem):
    return pltpu.CompilerParams(dimension_semantics=sem, vmem_limit_bytes=VMEM_LIMIT)


def _rms(x, w):
    return x * lax.rsqrt(jnp.mean(x * x, axis=-1, keepdims=True) + NORM_EPS) * w


def _silu(x):
    return x * jax.nn.sigmoid(x)


def _softplus(x):
    return jnp.maximum(x, 0.0) + jnp.log1p(jnp.exp(-jnp.abs(x)))


def _expm1(u):
    w = jnp.exp(u)
    near = (jnp.abs(u) < 0.5) & (w != 1.0)
    kahan = (w - 1.0) * u / jnp.log(jnp.where(near, w, 2.0))
    return jnp.where(near, kahan, jnp.where(w == 1.0, u, w - 1.0))


def _dot(a, b):
    return jnp.dot(a, b, preferred_element_type=F32)


def _dot_nt(a, b):
    return lax.dot_general(a, b, (((1,), (1,)), ((), ())), preferred_element_type=F32)


def _dot_tn(a, b):
    return lax.dot_general(a, b, (((0,), (0,)), ((), ())), preferred_element_type=F32)


def _sum01_matmul(m01, x):
    hi = x.astype(BF16)
    r1 = x - hi.astype(F32)
    mid = r1.astype(BF16)
    lo = (r1 - mid.astype(F32)).astype(BF16)
    return _dot(m01, hi) + _dot(m01, mid) + _dot(m01, lo)


def _sum01_matmul2(m01, x):
    hi = x.astype(BF16)
    mid = (x - hi.astype(F32)).astype(BF16)
    both = _dot(m01, jnp.concatenate([hi, mid], axis=1))
    n = x.shape[1]
    return both[:, :n] + both[:, n:]


def _row_scale(r, width):
    return jnp.concatenate([r] * (width // LANES), axis=1)


def _rinv(ssq, d):
    return lax.rsqrt(ssq * (1.0 / d) + NORM_EPS)


def _stream_prep_kernel(x_ref, xb_ref, r_ref):
    x = x_ref[...]
    xb_ref[...] = x.astype(BF16)
    r_ref[...] = jnp.broadcast_to(_rinv(jnp.sum(x * x, axis=-1, keepdims=True), x.shape[1]), r_ref.shape)


def _stream_prep(x2, *, tm=1024):
    t, d = x2.shape
    return pl.pallas_call(
        _stream_prep_kernel,
        grid=(t // tm,),
        in_specs=[pl.BlockSpec((tm, d), lambda i: (i, 0))],
        out_specs=[pl.BlockSpec((tm, d), lambda i: (i, 0)), pl.BlockSpec((tm, LANES), lambda i: (i, 0))],
        out_shape=[jax.ShapeDtypeStruct((t, d), BF16), jax.ShapeDtypeStruct((t, LANES), F32)],
        compiler_params=_cparams("parallel"),
        name="stream_prep",
    )(x2)


def _ffn_kernel(xb_ref, r_ref, xc_ref, wg_ref, wu_ref, wd_ref, o_ref, ob_ref, ro_ref, a_ref, ssq_ref,
                *, n_f, tf, d):
    j = pl.program_id(1)

    @pl.when(j < n_f)
    def _():
        xb = xb_ref[...]
        rt = _row_scale(r_ref[...], tf)
        a_ref[j] = (_silu(_dot(xb, wg_ref[...]) * rt) * (_dot(xb, wu_ref[...]) * rt)).astype(BF16)

    @pl.when(j >= n_f)
    def _():
        acc = _dot(a_ref[0], wd_ref[0:tf, :])
        for jj in range(1, n_f):
            acc = acc + _dot(a_ref[jj], wd_ref[jj * tf:(jj + 1) * tf, :])
        y = xc_ref[...] + 0.5 * acc
        o_ref[...] = y
        ob_ref[...] = y.astype(BF16)
        part = jnp.broadcast_to(jnp.sum(y * y, axis=-1, keepdims=True), ssq_ref.shape)
        ssq = part + jnp.where(j == n_f, 0.0, ssq_ref[...])
        ssq_ref[...] = ssq
        ro_ref[...] = _rinv(ssq, d)


FFN_TM = 1024
FFN_TF = 512
FFN_TN = 512


def _tile_cols(w, tile, gain=None):
    nl, k, n = w.shape
    if gain is not None:
        w = w * gain.astype(w.dtype)[:, :, None]
    return w.astype(BF16).reshape(nl, k, n // tile, tile).transpose(0, 2, 1, 3)


def _ffn(stream, wg_t, wu_t, wd_t, layer, *, tm=FFN_TM):
    x2, xb, r = stream
    t, d = x2.shape
    _, n_f, _, tf = wg_t.shape
    _, n_d, f, tn = wd_t.shape
    up = lambda i, j: (layer, jnp.minimum(j, n_f - 1), 0, 0)
    down = lambda i, j: (layer, jnp.maximum(j - n_f, 0), 0, 0)
    col = lambda i, j: (i, jnp.maximum(j - n_f, 0))
    rows = lambda i, j: (i, 0)
    return pl.pallas_call(
        functools.partial(_ffn_kernel, n_f=n_f, tf=tf, d=d),
        grid=(t // tm, n_f + n_d),
        in_specs=[
            pl.BlockSpec((tm, d), rows),
            pl.BlockSpec((tm, LANES), rows),
            pl.BlockSpec((tm, tn), col),
            pl.BlockSpec((None, None, d, tf), up),
            pl.BlockSpec((None, None, d, tf), up),
            pl.BlockSpec((None, None, f, tn), down),
        ],
        out_specs=[pl.BlockSpec((tm, tn), col), pl.BlockSpec((tm, tn), col), pl.BlockSpec((tm, LANES), rows)],
        out_shape=[jax.ShapeDtypeStruct((t, d), F32), jax.ShapeDtypeStruct((t, d), BF16),
                   jax.ShapeDtypeStruct((t, LANES), F32)],
        scratch_shapes=[pltpu.VMEM((n_f, tm, tf), BF16), pltpu.VMEM((tm, LANES), F32)],
        compiler_params=_cparams("parallel", "arbitrary"),
        name="ffn",
    )(xb, r, x2, wg_t, wu_t, wd_t)


def _final_norm_kernel(x_ref, w_ref, o_ref):
    o_ref[...] = _rms(x_ref[...], w_ref[...])


def _final_norm(x2, w, *, tm=1024):
    t, d = x2.shape
    return pl.pallas_call(
        _final_norm_kernel,
        grid=(t // tm,),
        in_specs=[pl.BlockSpec((tm, d), lambda i: (i, 0)), pl.BlockSpec((1, d), lambda i: (0, 0))],
        out_specs=pl.BlockSpec((tm, d), lambda i: (i, 0)),
        out_shape=jax.ShapeDtypeStruct((t, d), F32),
        compiler_params=_cparams("parallel"),
        name="final_norm",
    )(x2, w)


def _inproj_kernel(xb_ref, r_ref, w_ref, o_ref):
    r = r_ref[...]
    res = _dot(xb_ref[...], w_ref[...])
    for p in range(o_ref.shape[0]):
        o_ref[p] = res[:, p * LANES:(p + 1) * LANES] * r


INPROJ_TN = PROJ_W // 4


def _inproj(stream, w_t, layer, *, tm=1024):
    _, xb, r = stream
    t, d = xb.shape
    _, n_t, _, tn = w_t.shape
    return pl.pallas_call(
        _inproj_kernel,
        grid=(t // tm, n_t),
        in_specs=[
            pl.BlockSpec((tm, d), lambda i, j: (i, 0)),
            pl.BlockSpec((tm, LANES), lambda i, j: (i, 0)),
            pl.BlockSpec((None, None, d, tn), lambda i, j: (layer, j, 0, 0)),
        ],
        out_specs=pl.BlockSpec((tn // LANES, tm, LANES), lambda i, j: (j, i, 0)),
        out_shape=jax.ShapeDtypeStruct((PROJ_BLOCKS * SLABS, t, LANES), F32),
        compiler_params=_cparams("parallel", "arbitrary"),
        name="inproj",
    )(xb, r, w_t)


def _outproj_kernel(x_ref, ya_ref, yb_ref, yc_ref, yd_ref, bnw_ref, w_ref, o_ref, ob_ref, ro_ref):
    yb = _rms(yb_ref[...], bnw_ref[...]).astype(BF16)
    acc = x_ref[...]
    for g, y in enumerate((ya_ref[...], yb, yc_ref[...], yd_ref[...])):
        acc = acc + _dot(y, w_ref[g * GROUP_W:(g + 1) * GROUP_W, :])
    o_ref[...] = acc
    ob_ref[...] = acc.astype(BF16)
    ro_ref[...] = jnp.broadcast_to(_rinv(jnp.sum(acc * acc, axis=-1, keepdims=True), acc.shape[1]), ro_ref.shape)


def _outproj(x2, ya, yb, yc, yd, attn_norm_w, w, *, tm=512):
    t, d = x2.shape
    yspec = pl.BlockSpec((tm, GROUP_W), lambda i: (i, 0))
    rows = pl.BlockSpec((tm, d), lambda i: (i, 0))
    return pl.pallas_call(
        _outproj_kernel,
        grid=(t // tm,),
        in_specs=[rows, yspec, yspec, yspec, yspec,
                  pl.BlockSpec((1, GROUP_W), lambda i: (0, 0)), pl.BlockSpec((4 * GROUP_W, d), lambda i: (0, 0))],
        out_specs=[rows, rows, pl.BlockSpec((tm, LANES), lambda i: (i, 0))],
        out_shape=[jax.ShapeDtypeStruct((t, d), F32), jax.ShapeDtypeStruct((t, d), BF16),
                   jax.ShapeDtypeStruct((t, LANES), F32)],
        compiler_params=_cparams("parallel"),
        name="outproj",
    )(x2, ya, yb, yc, yd, attn_norm_w, w)


def _hgrn_tables(c=CHUNK, levels=CHUNK_LEVELS):
    t = np.arange(c)
    rows = []
    for lvl in range(levels):
        h = 1 << lvl
        r = (t // (2 * h)) * (2 * h) + h
        up = (t // h) % 2 == 1
        u = t[None, :]
        m = np.where(up[:, None], (u > r[:, None]) & (u <= t[:, None]), (u > t[:, None]) & (u <= r[:, None]))
        rows.append(m)
    rows.append(t[None, :] <= t[:, None])
    rows.append(t[None, :] > t[:, None])
    mstack = np.concatenate(rows, axis=0).astype(np.float32)
    x = t[:, None] ^ t[None, :]
    lv = np.where(t[None, :] < t[:, None], np.floor(np.log2(np.maximum(x, 1))).astype(np.int32), -1)
    return mstack, lv.astype(np.int32)


_HGRN_MSTACK, _HGRN_LEVEL = _hgrn_tables()


def _hgrn_kernel(q_ref, f_ref, i_ref, g_ref, lbl_ref, nw_ref, mst_ref, lv_ref, o_ref, st_ref, *, layer):
    c = CHUNK

    @pl.when(pl.program_id(1) == 0)
    def _():
        st_ref[...] = jnp.zeros_like(st_ref)

    lbl = lbl_ref[...]
    e = jnp.exp(lbl - jnp.max(lbl, axis=0, keepdims=True))
    p = e / jnp.sum(e, axis=0, keepdims=True)
    lb = jnp.zeros((1, GROUP_W), F32)
    for i in range(1, layer + 1):
        lb = lb + p[i:i + 1, :]

    lax.fori_loop(0, STEP_CHUNKS, functools.partial(
        _hgrn_chunk, q_ref, f_ref, i_ref, g_ref, nw_ref, mst_ref, lv_ref, o_ref, st_ref, lb), 0)


def _hgrn_chunk(q_ref, f_ref, i_ref, g_ref, nw_ref, mst_ref, lv_ref, o_ref, st_ref, lb, ci, carry):
    c = CHUNK
    rows = pl.ds(pl.multiple_of(ci * c, c), c)
    mst = mst_ref[...]
    lv = lv_ref[...]
    row = lax.broadcasted_iota(jnp.int32, (c, 1), 0)
    for h in range(HG_HEADS):
        sl = slice(h * HG_HEAD_W, (h + 1) * HG_HEAD_W)
        q = _silu(q_ref[h, rows, :])
        z = f_ref[h, rows, :]
        v = i_ref[h, rows, :]
        lbh = lb[:, sl]
        g = jnp.log(lbh + (1.0 - lbh) * jax.nn.sigmoid(z))
        k = (1.0 - lbh) * jax.nn.sigmoid(-z)
        ex = jnp.exp(_sum01_matmul2(mst, g))
        a = jnp.zeros((c, c), F32)
        for lvl in range(CHUNK_LEVELS):
            exl = ex[lvl * c:(lvl + 1) * c]
            up = ((row >> lvl) & 1) == 1
            qp = jnp.where(up, q * exl, 0.0).astype(BF16)
            kp = jnp.where(up, 0.0, k * exl).astype(BF16)
            a = jnp.where(lv == lvl, _dot_nt(qp, kp), a)
        vb = v.astype(BF16)
        ex_b = ex[CHUNK_LEVELS * c:(CHUNK_LEVELS + 1) * c]
        ex_r = ex[(CHUNK_LEVELS + 1) * c:(CHUNK_LEVELS + 2) * c]
        st = st_ref[h]
        o = (_dot(a.astype(BF16), vb) + jnp.sum(q * k, axis=-1, keepdims=True) * v
             + _dot_nt((q * ex_b).astype(BF16), st.astype(BF16)))
        st_ref[h] = ex_b[c - 1:c, :] * st + _dot_tn(vb, (k * ex_r).astype(BF16))
        o = _rms(o, nw_ref[:, sl]) * _silu(g_ref[h, rows, :])
        o_ref[rows, sl] = o.astype(o_ref.dtype)
    return carry


def _slab_spec(blk, rows, n=SLABS):
    first = blk * SLABS // n
    return pl.BlockSpec((n, None, rows, LANES), lambda bi, ci: (first, bi, ci, 0))


def _hgrn(proj4, lb_logits, norm_w, *, layer):
    _, b, s, _ = proj4.shape
    cspec = lambda blk: _slab_spec(blk, STEP_ROWS)
    full = lambda shape: pl.BlockSpec(shape, lambda bi, ci: (0,) * len(shape))
    mst = jnp.asarray(_HGRN_MSTACK, BF16)
    lv = jnp.asarray(_HGRN_LEVEL)
    return pl.pallas_call(
        functools.partial(_hgrn_kernel, layer=layer),
        grid=(b, s // STEP_ROWS),
        in_specs=[cspec(BLK_A), cspec(BLK_A + 1), cspec(BLK_A + 2), cspec(BLK_A + 3),
                  full(lb_logits.shape), full((1, GROUP_W)), full(mst.shape), full(lv.shape)],
        out_specs=pl.BlockSpec((None, STEP_ROWS, GROUP_W), lambda bi, ci: (bi, ci, 0)),
        out_shape=jax.ShapeDtypeStruct((b, s, GROUP_W), BF16),
        scratch_shapes=[pltpu.VMEM((HG_HEADS, HG_HEAD_W, HG_HEAD_W), F32)],
        compiler_params=_cparams("parallel", "arbitrary"),
        name="hgrn2",
    )(proj4, proj4, proj4, proj4, lb_logits, norm_w, mst, lv)


ATT_TILE = 2048
ATT_GROUP = 4
ATT_MID = 4
assert ATT_DILATIONS == (1, ATT_MID, ATT_MID * ATT_MID) and ATT_TILE == ATT_SPAN * ATT_DILATIONS[-1]


def _attend_group(qs, ks, vs, first_flags, bias_all, bias_first, first_head):
    blk = ATT_SPAN
    scores = []
    for q2, k2, ff in zip(qs, ks, first_flags):
        qq = jnp.concatenate([jnp.where(first_head, q2, 0.0), jnp.where(first_head, 0.0, q2)], axis=0).astype(BF16)
        bias = jnp.where(ff > 0, bias_first, bias_all)
        scores.append(_dot_nt(qq, k2) + jnp.concatenate([bias, bias], axis=0))
    s = jnp.concatenate(scores, axis=0)
    m = jnp.max(s, axis=-1, keepdims=True)
    p = jnp.exp(s - m)
    den = jnp.sum(p, axis=-1, keepdims=True)
    lse = m + jnp.log(den)
    pb = p.astype(BF16)
    outs = []
    for g, v2 in enumerate(vs):
        r0 = g * 2 * blk
        pv = _dot(pb[r0:r0 + 2 * blk], v2)
        d0, d1 = den[r0:r0 + blk], den[r0 + blk:r0 + 2 * blk]
        l0, l1 = lse[r0:r0 + blk], lse[r0 + blk:r0 + 2 * blk]
        o2 = jnp.where(first_head, pv[:blk], pv[blk:]) / jnp.where(first_head, d0, d1)
        outs.append((o2, jnp.where(first_head, l0, l1)))
    return outs


def _attn_kernel(q_ref, k_ref, v_ref, y_ref, q4_ref, k4_ref, v4_ref, kd1, vd1, kd4, vd4, kd16, vd16, op_ref, lp_ref):
    i = pl.program_id(2)
    blk, tile, mid = ATT_SPAN, ATT_TILE, ATT_MID
    len4 = tile // mid
    pitch4 = blk + len4
    pitch16 = 2 * blk
    scale = ATT_HEAD_DIM ** -0.5
    zeros = jnp.zeros((blk, LANES), BF16)

    @pl.when(i == 0)
    def _():
        for kd, vd, pitch, nres in ((kd1, vd1, 0, 1), (kd4, vd4, pitch4, mid), (kd16, vd16, pitch16, mid * mid)):
            for r in range(nres):
                kd[r * pitch:r * pitch + blk, :] = zeros
                vd[r * pitch:r * pitch + blk, :] = zeros

    kd1[blk:blk + tile, :] = k_ref[...].astype(BF16)
    vd1[blk:blk + tile, :] = v_ref[...].astype(BF16)
    for r in range(mid):
        rows = slice(r * len4, (r + 1) * len4)
        cur = slice(r * pitch4 + blk, (r + 1) * pitch4)
        kk = k_ref[pl.ds(r, len4, stride=mid), :]
        vv = v_ref[pl.ds(r, len4, stride=mid), :]
        k4_ref[rows, :] = kk
        v4_ref[rows, :] = vv
        kd4[cur, :] = kk.astype(BF16)
        vd4[cur, :] = vv.astype(BF16)
        q4_ref[rows, :] = q_ref[pl.ds(r, len4, stride=mid), :] * scale
    for r4 in range(mid):
        for hi in range(mid):
            r16 = r4 + mid * hi
            cur = slice(r16 * pitch16 + blk, (r16 + 1) * pitch16)
            kd16[cur, :] = k4_ref[pl.ds(r4 * len4 + hi, blk, stride=mid), :].astype(BF16)
            vd16[cur, :] = v4_ref[pl.ds(r4 * len4 + hi, blk, stride=mid), :].astype(BF16)

    iq = lax.broadcasted_iota(jnp.int32, (blk, 2 * blk), 0)
    ik = lax.broadcasted_iota(jnp.int32, (blk, 2 * blk), 1)
    dist = iq + blk - ik
    valid = (dist >= 0) & (dist <= blk)
    bias_all = jnp.where(valid, 0.0, MASK_VALUE)
    bias_first = jnp.where(valid & (ik >= blk), 0.0, MASK_VALUE)
    lane = lax.broadcasted_iota(jnp.int32, (1, LANES), 1)
    first_head = lane < ATT_HEAD_DIM
    tile0 = 1 - jnp.minimum(i, 1)
    grp = ATT_GROUP

    def attend(qs, kd, vd, bases, flags):
        ks = [kd[pl.ds(pl.multiple_of(b0, blk), 2 * blk), :] for b0 in bases]
        vs = [vd[pl.ds(pl.multiple_of(b0, blk), 2 * blk), :] for b0 in bases]
        return _attend_group(qs, ks, vs, flags, bias_all, bias_first, first_head)

    def pattern1(it, carry):
        b0s = [(it * grp + g) * blk for g in range(grp)]
        qs = [q_ref[pl.ds(pl.multiple_of(b0, blk), blk), :] * scale for b0 in b0s]
        flags = [tile0 * (1 - jnp.minimum(it * grp + g, 1)) for g in range(grp)]
        for b0, (o2, l2) in zip(b0s, attend(qs, kd1, vd1, b0s, flags)):
            op_ref[0, pl.ds(pl.multiple_of(b0, blk), blk), :] = o2
            lp_ref[0, pl.ds(pl.multiple_of(b0, blk), blk), :] = l2
        return carry

    def pattern4(r, carry):
        qs = [q4_ref[pl.ds(pl.multiple_of(r * len4 + g * blk, blk), blk), :] for g in range(grp)]
        bases = [r * pitch4 + g * blk for g in range(grp)]
        flags = [tile0 if g == 0 else 0 * tile0 for g in range(grp)]
        for g, (o2, l2) in enumerate(attend(qs, kd4, vd4, bases, flags)):
            op_ref[1, pl.ds(r + mid * blk * g, blk, stride=mid), :] = o2
            lp_ref[1, pl.ds(r + mid * blk * g, blk, stride=mid), :] = l2
        return carry

    def pattern16(hi, carry):
        qs = [q4_ref[pl.ds(g * len4 + hi, blk, stride=mid), :] for g in range(grp)]
        bases = [(g + mid * hi) * pitch16 for g in range(grp)]
        flags = [tile0] * grp
        for g, (o2, l2) in enumerate(attend(qs, kd16, vd16, bases, flags)):
            op_ref[2, pl.ds(g + mid * hi, blk, stride=mid * mid), :] = o2
            lp_ref[2, pl.ds(g + mid * hi, blk, stride=mid * mid), :] = l2
        return carry

    assert grp == mid and tile // blk == grp * mid
    lax.fori_loop(0, mid, pattern1, 0)
    lax.fori_loop(0, mid, pattern4, 0)
    lax.fori_loop(0, mid, pattern16, 0)

    for kd, vd, pitch, nres, ln in ((kd1, vd1, 0, 1, tile), (kd4, vd4, pitch4, mid, len4),
                                    (kd16, vd16, pitch16, mid * mid, blk)):
        for r in range(nres):
            kd[r * pitch:r * pitch + blk, :] = kd[r * pitch + ln:r * pitch + ln + blk, :]
            vd[r * pitch:r * pitch + blk, :] = vd[r * pitch + ln:r * pitch + ln + blk, :]

    def merge(c, carry):
        rows = pl.ds(pl.multiple_of(c * CHUNK, CHUNK), CHUNK)
        la, lb, lc = lp_ref[0, rows, :], lp_ref[1, rows, :], lp_ref[2, rows, :]
        m = jnp.maximum(jnp.maximum(la, lb), lc)
        wa, wb, wc = jnp.exp(la - m), jnp.exp(lb - m), jnp.exp(lc - m)
        y_ref[rows, :] = (wa * op_ref[0, rows, :] + wb * op_ref[1, rows, :] + wc * op_ref[2, rows, :]) / (wa + wb + wc)
        return carry

    lax.fori_loop(0, tile // CHUNK, merge, 0)


def _attn(proj4):
    _, b, s, _ = proj4.shape
    tile, blk, mid = ATT_TILE, ATT_SPAN, ATT_MID
    spec = lambda off: pl.BlockSpec((None, None, tile, LANES),
                                    lambda bi, pr, ti: ((BLK_B + off) * SLABS + pr, bi, ti, 0))
    return pl.pallas_call(
        _attn_kernel,
        grid=(b, SLABS, s // tile),
        in_specs=[spec(0), spec(1), spec(2)],
        out_specs=pl.BlockSpec((None, tile, LANES), lambda bi, pr, ti: (bi, ti, pr)),
        out_shape=jax.ShapeDtypeStruct((b, s, GROUP_W), F32),
        scratch_shapes=[pltpu.VMEM((tile, LANES), F32)] * 3
        + [pltpu.VMEM((blk + tile, LANES), BF16)] * 2
        + [pltpu.VMEM((mid * (blk + tile // mid), LANES), BF16)] * 2
        + [pltpu.VMEM((mid * mid * 2 * blk, LANES), BF16)] * 2
        + [pltpu.VMEM((len(ATT_DILATIONS), tile, LANES), F32)] * 2,
        compiler_params=_cparams("parallel", "parallel", "arbitrary"),
        name="dilattn",
    )(proj4, proj4, proj4)


def _causal_conv(x_ref, rows, xpad_ref, w_ref, b_ref):
    c = CHUNK
    for p in range(x_ref.shape[0]):
        xpad_ref[SUBLANES:SUBLANES + c, p * LANES:(p + 1) * LANES] = x_ref[p, rows, :]
    acc = b_ref[...] + w_ref[0:1, :] * xpad_ref[pl.ds(SUBLANES - CONV_W + 1, c), :]
    for j in range(1, CONV_W):
        acc = acc + w_ref[j:j + 1, :] * xpad_ref[pl.ds(SUBLANES - CONV_W + 1 + j, c), :]
    xpad_ref[0:SUBLANES, :] = xpad_ref[c:c + SUBLANES, :]
    return acc


def _pair_lanes(col_fn, first_head):
    return jnp.where(first_head, col_fn(0), col_fn(1))


def _ssd_kernel(z_ref, xbc_ref, dt_ref, cw_ref, cb_ref, dtb_ref, alog_ref, dsk_ref, nw_ref, tril_ref,
                y_ref, xpad_ref, st_ref):
    @pl.when(pl.program_id(1) == 0)
    def _():
        st_ref[...] = jnp.zeros_like(st_ref)
        xpad_ref[0:SUBLANES, :] = jnp.zeros((SUBLANES, xpad_ref.shape[1]), F32)

    lax.fori_loop(0, STEP_CHUNKS, functools.partial(
        _ssd_chunk, z_ref, xbc_ref, dt_ref, cw_ref, cb_ref, dtb_ref, alog_ref, dsk_ref, nw_ref, tril_ref,
        y_ref, xpad_ref, st_ref), 0)


def _ssd_chunk(z_ref, xbc_ref, dt_ref, cw_ref, cb_ref, dtb_ref, alog_ref, dsk_ref, nw_ref, tril_ref,
               y_ref, xpad_ref, st_ref, ci, carry):
    c = CHUNK
    rows = pl.ds(pl.multiple_of(ci * c, c), c)
    xbc = _silu(_causal_conv(xbc_ref, rows, xpad_ref, cw_ref, cb_ref))
    dt = _softplus(dt_ref[rows, :] + dtb_ref[...])
    adt = dt * (-jnp.exp(alog_ref[...]))
    tril01 = tril_ref[...]
    acs = _sum01_matmul(tril01, adt)
    acs_t = acs.T
    e_cs = jnp.exp(acs)
    dt_rest = dt * jnp.exp(acs[c - 1:c, :] - acs)
    causal = tril01 > 0

    lane = lax.broadcasted_iota(jnp.int32, (1, LANES), 1)
    first_head = lane < SSM_HEAD_DIM
    gw = GROUP_W // SSM_GROUPS
    hpg = SSM_HEADS // SSM_GROUPS
    for g in range(SSM_GROUPS):
        bg = xbc[:, GROUP_W + g * SSM_STATE:GROUP_W + (g + 1) * SSM_STATE]
        cg = xbc[:, GROUP_W + (SSM_GROUPS + g) * SSM_STATE:GROUP_W + (SSM_GROUPS + g + 1) * SSM_STATE].astype(BF16)
        cb = jnp.where(causal, _dot_nt(cg, bg.astype(BF16)), 0.0)
        st = st_ref[g]
        y_off = _dot(cg, st.astype(BF16))
        ys, xds, dec_last = [], [], []
        for pr in range(hpg // 2):
            h0 = g * hpg + 2 * pr
            ls = slice(h0 * SSM_HEAD_DIM, (h0 + 2) * SSM_HEAD_DIM)
            xs = xbc[:, ls]
            xdt = (xs * _pair_lanes(lambda i: dt[:, h0 + i:h0 + i + 1], first_head)).astype(BF16)
            yd = []
            for i in range(2):
                hh = h0 + i
                seg = jnp.minimum(acs[:, hh:hh + 1] - acs_t[hh:hh + 1, :], 0.0)
                yd.append(_dot((cb * jnp.exp(seg)).astype(BF16), xdt))
            y_pair = (jnp.where(first_head, yd[0], yd[1])
                      + y_off[:, pr * LANES:(pr + 1) * LANES] * _pair_lanes(lambda i: e_cs[:, h0 + i:h0 + i + 1], first_head)
                      + dsk_ref[:, ls] * xs)
            ys.append(y_pair)
            xds.append((xs * _pair_lanes(lambda i: dt_rest[:, h0 + i:h0 + i + 1], first_head)).astype(BF16))
            dec_last.append(_pair_lanes(lambda i: e_cs[c - 1:c, h0 + i:h0 + i + 1], first_head))
        st_ref[g] = (jnp.concatenate(dec_last, axis=1) * st
                     + _dot_tn(bg.astype(BF16), jnp.concatenate(xds, axis=1)))
        zg = jnp.concatenate([z_ref[g * (gw // LANES) + p, rows, :] for p in range(gw // LANES)], axis=1)
        yg = jnp.concatenate(ys, axis=1) * _silu(zg)
        y_ref[rows, g * gw:(g + 1) * gw] = _rms(yg, nw_ref[:, g * gw:(g + 1) * gw]).astype(y_ref.dtype)
    return carry


def _ssd(proj4, conv_w, conv_b, dt_bias, a_log, d_skip, norm_w):
    _, b, s, _ = proj4.shape
    full = lambda shape: pl.BlockSpec(shape, lambda bi, ci: (0,) * len(shape))
    tril = jnp.asarray(np.tril(np.ones((CHUNK, CHUNK), np.float32)), BF16)
    return pl.pallas_call(
        _ssd_kernel,
        grid=(b, s // STEP_ROWS),
        in_specs=[
            _slab_spec(BLK_CZ, STEP_ROWS),
            _slab_spec(BLK_CXBC, STEP_ROWS, n=SSM_CONV_DIM // LANES),
            pl.BlockSpec((None, None, STEP_ROWS, LANES), lambda bi, ci: (BLK_CDT * SLABS, bi, ci, 0)),
            full((CONV_W, SSM_CONV_DIM)), full((1, SSM_CONV_DIM)), full((1, LANES)), full((1, LANES)),
            full((1, GROUP_W)), full((1, GROUP_W)), full((CHUNK, CHUNK)),
        ],
        out_specs=pl.BlockSpec((None, STEP_ROWS, GROUP_W), lambda bi, ci: (bi, ci, 0)),
        out_shape=jax.ShapeDtypeStruct((b, s, GROUP_W), BF16),
        scratch_shapes=[pltpu.VMEM((CHUNK + SUBLANES, SSM_CONV_DIM), F32),
                        pltpu.VMEM((SSM_GROUPS, SSM_STATE, GROUP_W // SSM_GROUPS), F32)],
        compiler_params=_cparams("parallel", "arbitrary"),
        name="ssd",
    )(proj4, proj4, proj4, conv_w, conv_b, dt_bias, a_log, d_skip, norm_w, tril)


def _gelu_tanh(x):
    return 0.5 * x * (1.0 + jnp.tanh(np.sqrt(2.0 / np.pi).astype(np.float32) * (x + 0.044715 * (x * x * x))))


def _lru_kernel(x_ref, g_ref, cw_ref, cb_ref, wa_ref, ba_ref, wx_ref, bx_ref, ap_ref, nw_ref,
                y_ref, xpad_ref, h_ref):
    @pl.when(pl.program_id(1) == 0)
    def _():
        h_ref[...] = jnp.zeros_like(h_ref)
        xpad_ref[0:SUBLANES, :] = jnp.zeros((SUBLANES, xpad_ref.shape[1]), F32)

    lax.fori_loop(0, STEP_CHUNKS, functools.partial(
        _lru_chunk, x_ref, g_ref, cw_ref, cb_ref, wa_ref, ba_ref, wx_ref, bx_ref, ap_ref, nw_ref,
        y_ref, xpad_ref, h_ref), 0)


def _lru_chunk(x_ref, g_ref, cw_ref, cb_ref, wa_ref, ba_ref, wx_ref, bx_ref, ap_ref, nw_ref,
               y_ref, xpad_ref, h_ref, ci, carry):
    c = CHUNK
    rows = pl.ds(pl.multiple_of(ci * c, c), c)
    xc = _causal_conv(x_ref, rows, xpad_ref, cw_ref, cb_ref)
    xb = xc.astype(BF16)
    r = jax.nn.sigmoid(_dot(xb, wa_ref[...]) + ba_ref[...])
    i = jax.nn.sigmoid(_dot(xb, wx_ref[...]) + bx_ref[...])
    log_a = -LRU_C * r * _softplus(-ap_ref[...])
    a = jnp.exp(log_a)
    bt = jnp.sqrt(jnp.maximum(-_expm1(2.0 * log_a), 0.0)) * (i * xc)
    row = lax.broadcasted_iota(jnp.int32, (c, 1), 0)
    d = 1
    while d < c:
        keep = row >= d
        a_s = pltpu.roll(a, d, 0)
        b_s = pltpu.roll(bt, d, 0)
        bt = jnp.where(keep, a * b_s + bt, bt)
        a = jnp.where(keep, a * a_s, a)
        d *= 2
    h = bt + a * h_ref[...]
    h_ref[...] = h[c - 1:c, :]
    gate = jnp.concatenate([g_ref[p, rows, :] for p in range(SLABS)], axis=1)
    y_ref[rows, :] = _rms(h * _gelu_tanh(gate), nw_ref[...]).astype(y_ref.dtype)
    return carry


def _lru(proj4, conv_w, conv_b, wa_bd, b_a, wx_bd, b_x, a_param, norm_w):
    _, b, s, _ = proj4.shape
    full = lambda shape: pl.BlockSpec(shape, lambda bi, ci: (0,) * len(shape))
    vec = full((1, GROUP_W))
    return pl.pallas_call(
        _lru_kernel,
        grid=(b, s // STEP_ROWS),
        in_specs=[
            _slab_spec(BLK_DX, STEP_ROWS),
            _slab_spec(BLK_DG, STEP_ROWS),
            full((CONV_W, GROUP_W)), vec, full((GROUP_W, GROUP_W)), vec, full((GROUP_W, GROUP_W)), vec, vec, vec,
        ],
        out_specs=pl.BlockSpec((None, STEP_ROWS, GROUP_W), lambda bi, ci: (bi, ci, 0)),
        out_shape=jax.ShapeDtypeStruct((b, s, GROUP_W), BF16),
        scratch_shapes=[pltpu.VMEM((CHUNK + SUBLANES, GROUP_W), F32), pltpu.VMEM((1, GROUP_W), F32)],
        compiler_params=_cparams("parallel", "arbitrary"),
        name="rglru",
    )(proj4, proj4, conv_w, conv_b, wa_bd, b_a, wx_bd, b_x, a_param, norm_w)


def _block_diag(w):
    nb, n, _ = w.shape
    eye = jnp.eye(nb, dtype=w.dtype)
    return jnp.einsum("hij,hg->higj", w, eye).reshape(nb * n, nb * n)


def _pad_lanes(v, width=LANES):
    return jnp.pad(v, (0, width - v.shape[0])).reshape(1, width)


def kernel(x, ffn1_norm, ffn1_w_gate, ffn1_w_up, ffn1_w_down, mix_norm, w_in, w_out, hgrn_lb_logits, hgrn_norm, attn_norm, ssm_conv_w, ssm_conv_b, ssm_dt_bias, ssm_a_log, ssm_d, ssm_norm, lru_conv_w, lru_conv_b, lru_w_a, lru_b_a, lru_w_x, lru_b_x, lru_a_param, lru_norm, ffn2_norm, ffn2_w_gate, ffn2_w_up, ffn2_w_down, final_norm):
    bsz, seq, d = x.shape
    depth = w_in.shape[0]
    t = bsz * seq
    row = lambda v: v.reshape(1, -1).astype(F32)
    x2 = x.reshape(t, d)
    lb_logits = hgrn_lb_logits.astype(F32)
    ffn1_w = (_tile_cols(ffn1_w_gate, FFN_TF, ffn1_norm), _tile_cols(ffn1_w_up, FFN_TF, ffn1_norm),
              _tile_cols(ffn1_w_down, FFN_TN))
    ffn2_w = (_tile_cols(ffn2_w_gate, FFN_TF, ffn2_norm), _tile_cols(ffn2_w_up, FFN_TF, ffn2_norm),
              _tile_cols(ffn2_w_down, FFN_TN))
    w_in_t = _tile_cols(jnp.concatenate(
        [w_in[:, :, :N_IN_BEFORE_PAD],
         jnp.zeros((depth, d, (BLK_DX * GROUP_W) - N_IN_BEFORE_PAD), w_in.dtype),
         w_in[:, :, N_IN_BEFORE_PAD:]], axis=2), INPROJ_TN, mix_norm)
    stream = (x2, *_stream_prep(x2))
    for l in range(depth):
        stream = _ffn(stream, *ffn1_w, l)
        x2 = stream[0]
        proj4 = _inproj(stream, w_in_t, l).reshape(PROJ_BLOCKS * SLABS, bsz, seq, LANES)
        y_a = _hgrn(proj4, lb_logits, row(hgrn_norm[l]), layer=l).reshape(t, GROUP_W)
        y_b = _attn(proj4).reshape(t, GROUP_W)
        y_c = _ssd(proj4, ssm_conv_w[l].astype(F32), row(ssm_conv_b[l]), _pad_lanes(ssm_dt_bias[l].astype(F32)),
                   _pad_lanes(ssm_a_log[l].astype(F32)), row(jnp.repeat(ssm_d[l], SSM_HEAD_DIM)),
                   row(ssm_norm[l])).reshape(t, GROUP_W)
        y_d = _lru(proj4, lru_conv_w[l].astype(F32), row(lru_conv_b[l]), _block_diag(lru_w_a[l]).astype(BF16),
                   row(lru_b_a[l]), _block_diag(lru_w_x[l]).astype(BF16), row(lru_b_x[l]), row(lru_a_param[l]),
                   row(lru_norm[l])).reshape(t, GROUP_W)
        stream = _outproj(x2, y_a, y_b, y_c, y_d, row(attn_norm[l]), w_out[l].astype(BF16))
        stream = _ffn(stream, *ffn2_w, l)
    return _final_norm(stream[0], row(final_norm)).reshape(bsz, seq, d)
```

```python
import functools

import numpy as np
import jax
import jax.numpy as jnp
from jax import lax
from jax.experimental import pallas as pl
from jax.experimental.pallas import tpu as pltpu

F32 = jnp.float32
BF16 = jnp.bfloat16

NORM_EPS = 1e-6
MASK_VALUE = -1e30
GROUP_W = 512
HG_HEADS = 4
HG_HEAD_W = GROUP_W // HG_HEADS
ATT_HEAD_DIM = 64
ATT_SPAN = 128
ATT_DILATIONS = (1, 4, 16)
SSM_HEADS = 8
SSM_HEAD_DIM = 64
SSM_GROUPS = 2
SSM_STATE = 128
SSM_CONV_DIM = GROUP_W + 2 * SSM_GROUPS * SSM_STATE
CONV_W = 4
LRU_C = 8.0

LANES = 128
SUBLANES = 8
VMEM_LIMIT = 60 * 1024 * 1024

PROJ_BLOCKS = 13
PROJ_W = PROJ_BLOCKS * GROUP_W
BLK_A = 0
BLK_B = 4
BLK_CZ = 7
BLK_CXBC = 8
BLK_CDT = 10
BLK_DX = 11
BLK_DG = 12
N_IN_BEFORE_PAD = 10 * GROUP_W + SSM_HEADS
SLABS = GROUP_W // LANES

CHUNK = 256
CHUNK_LEVELS = 8
STEP_CHUNKS = 2
STEP_ROWS = STEP_CHUNKS * CHUNK


def _cparams(*sem):
    return pltpu.CompilerParams(dimension_semantics=sem, vmem_limit_bytes=VMEM_LIMIT)


def _rms(x, w):
    return x * lax.rsqrt(jnp.mean(x * x, axis=-1, keepdims=True) + NORM_EPS) * w


def _silu(x):
    return x * jax.nn.sigmoid(x)


def _softplus(x):
    return jnp.maximum(x, 0.0) + jnp.log1p(jnp.exp(-jnp.abs(x)))


def _expm1(u):
    w = jnp.exp(u)
    near = (jnp.abs(u) < 0.5) & (w != 1.0)
    kahan = (w - 1.0) * u / jnp.log(jnp.where(near, w, 2.0))
    return jnp.where(near, kahan, jnp.where(w == 1.0, u, w - 1.0))


def _dot(a, b):
    return jnp.dot(a, b, preferred_element_type=F32)


def _dot_nt(a, b):
    return lax.dot_general(a, b, (((1,), (1,)), ((), ())), preferred_element_type=F32)


def _dot_tn(a, b):
    return lax.dot_general(a, b, (((0,), (0,)), ((), ())), preferred_element_type=F32)


def _sum01_matmul(m01, x):
    hi = x.astype(BF16)
    r1 = x - hi.astype(F32)
    mid = r1.astype(BF16)
    lo = (r1 - mid.astype(F32)).astype(BF16)
    return _dot(m01, hi) + _dot(m01, mid) + _dot(m01, lo)


def _sum01_matmul2(m01, x):
    hi = x.astype(BF16)
    mid = (x - hi.astype(F32)).astype(BF16)
    both = _dot(m01, jnp.concatenate([hi, mid], axis=1))
    n = x.shape[1]
    return both[:, :n] + both[:, n:]


def _row_scale(r, width):
    return jnp.concatenate([r] * (width // LANES), axis=1)


def _rinv(ssq, d):
    return lax.rsqrt(ssq * (1.0 / d) + NORM_EPS)


def _stream_prep_kernel(x_ref, xb_ref, r_ref):
    x = x_ref[...]
    xb_ref[...] = x.astype(BF16)
    r_ref[...] = jnp.broadcast_to(_rinv(jnp.sum(x * x, axis=-1, keepdims=True), x.shape[1]), r_ref.shape)


def _stream_prep(x2, *, tm=1024):
    t, d = x2.shape
    return pl.pallas_call(
        _stream_prep_kernel,
        grid=(t // tm,),
        in_specs=[pl.BlockSpec((tm, d), lambda i: (i, 0))],
        out_specs=[pl.BlockSpec((tm, d), lambda i: (i, 0)), pl.BlockSpec((tm, LANES), lambda i: (i, 0))],
        out_shape=[jax.ShapeDtypeStruct((t, d), BF16), jax.ShapeDtypeStruct((t, LANES), F32)],
        compiler_params=_cparams("parallel"),
        name="stream_prep",
    )(x2)


def _ffn_kernel(xb_ref, r_ref, xc_ref, wg_ref, wu_ref, wd_ref, o_ref, ob_ref, ro_ref, a_ref, ssq_ref,
                *, n_f, tf, d):
    j = pl.program_id(1)

    @pl.when(j < n_f)
    def _():
        xb = xb_ref[...]
        rt = _row_scale(r_ref[...], tf)
        a_ref[j] = (_silu(_dot(xb, wg_ref[...]) * rt) * (_dot(xb, wu_ref[...]) * rt)).astype(BF16)

    @pl.when(j >= n_f)
    def _():
        acc = _dot(a_ref[0], wd_ref[0:tf, :])
        for jj in range(1, n_f):
            acc = acc + _dot(a_ref[jj], wd_ref[jj * tf:(jj + 1) * tf, :])
        y = xc_ref[...] + 0.5 * acc
        o_ref[...] = y
        ob_ref[...] = y.astype(BF16)
        part = jnp.broadcast_to(jnp.sum(y * y, axis=-1, keepdims=True), ssq_ref.shape)
        ssq = part + jnp.where(j == n_f, 0.0, ssq_ref[...])
        ssq_ref[...] = ssq
        ro_ref[...] = _rinv(ssq, d)


FFN_TM = 1024
FFN_TF = 512
FFN_TN = 512


def _tile_cols_kernel(w_ref, g_ref, o_ref):
    o_ref[...] = (w_ref[...] * g_ref[...]).astype(BF16)


def _tile_cols(w, tile, gain=None):
    nl, k, n = w.shape
    kb = k // 2
    g = jnp.ones((nl, k), F32) if gain is None else gain.astype(F32)
    return pl.pallas_call(
        _tile_cols_kernel,
        grid=(nl, n // tile, k // kb),
        in_specs=[pl.BlockSpec((None, kb, tile), lambda l, j, kk: (l, kk, j)),
                  pl.BlockSpec((None, kb, 1), lambda l, j, kk: (l, kk, 0))],
        out_specs=pl.BlockSpec((None, None, kb, tile), lambda l, j, kk: (l, j, kk, 0)),
        out_shape=jax.ShapeDtypeStruct((nl, n // tile, k, tile), BF16),
        compiler_params=_cparams("parallel", "parallel", "parallel"),
        name="tile_cols",
    )(w.astype(F32), g.reshape(nl, k, 1))


def _ffn(stream, wg_t, wu_t, wd_t, layer, *, tm=FFN_TM):
    x2, xb, r = stream
    t, d = x2.shape
    _, n_f, _, tf = wg_t.shape
    _, n_d, f, tn = wd_t.shape
    up = lambda i, j: (layer, jnp.minimum(j, n_f - 1), 0, 0)
    down = lambda i, j: (layer, jnp.maximum(j - n_f, 0), 0, 0)
    col = lambda i, j: (i, jnp.maximum(j - n_f, 0))
    rows = lambda i, j: (i, 0)
    return pl.pallas_call(
        functools.partial(_ffn_kernel, n_f=n_f, tf=tf, d=d),
        grid=(t // tm, n_f + n_d),
        in_specs=[
            pl.BlockSpec((tm, d), rows),
            pl.BlockSpec((tm, LANES), rows),
            pl.BlockSpec((tm, tn), col),
            pl.BlockSpec((None, None, d, tf), up),
            pl.BlockSpec((None, None, d, tf), up),
            pl.BlockSpec((None, None, f, tn), down),
        ],
        out_specs=[pl.BlockSpec((tm, tn), col), pl.BlockSpec((tm, tn), col), pl.BlockSpec((tm, LANES), rows)],
        out_shape=[jax.ShapeDtypeStruct((t, d), F32), jax.ShapeDtypeStruct((t, d), BF16),
                   jax.ShapeDtypeStruct((t, LANES), F32)],
        scratch_shapes=[pltpu.VMEM((n_f, tm, tf), BF16), pltpu.VMEM((tm, LANES), F32)],
        compiler_params=_cparams("parallel", "arbitrary"),
        name="ffn",
    )(xb, r, x2, wg_t, wu_t, wd_t)


def _final_norm_kernel(x_ref, w_ref, o_ref):
    o_ref[...] = _rms(x_ref[...], w_ref[...])


def _final_norm(x2, w, *, tm=1024):
    t, d = x2.shape
    return pl.pallas_call(
        _final_norm_kernel,
        grid=(t // tm,),
        in_specs=[pl.BlockSpec((tm, d), lambda i: (i, 0)), pl.BlockSpec((1, d), lambda i: (0, 0))],
        out_specs=pl.BlockSpec((tm, d), lambda i: (i, 0)),
        out_shape=jax.ShapeDtypeStruct((t, d), F32),
        compiler_params=_cparams("parallel"),
        name="final_norm",
    )(x2, w)


def _inproj_kernel(xb_ref, r_ref, w_ref, o_ref):
    r = r_ref[...]
    res = _dot(xb_ref[...], w_ref[...])
    for p in range(o_ref.shape[0]):
        o_ref[p] = res[:, p * LANES:(p + 1) * LANES] * r


INPROJ_TN = PROJ_W // 4


def _inproj(stream, w_t, layer, *, tm=1024):
    _, xb, r = stream
    t, d = xb.shape
    _, n_t, _, tn = w_t.shape
    return pl.pallas_call(
        _inproj_kernel,
        grid=(t // tm, n_t),
        in_specs=[
            pl.BlockSpec((tm, d), lambda i, j: (i, 0)),
            pl.BlockSpec((tm, LANES), lambda i, j: (i, 0)),
            pl.BlockSpec((None, None, d, tn), lambda i, j: (layer, j, 0, 0)),
        ],
        out_specs=pl.BlockSpec((tn // LANES, tm, LANES), lambda i, j: (j, i, 0)),
        out_shape=jax.ShapeDtypeStruct((PROJ_BLOCKS * SLABS, t, LANES), F32),
        compiler_params=_cparams("parallel", "arbitrary"),
        name="inproj",
    )(xb, r, w_t)


def _outproj_kernel(x_ref, ya_ref, yb_ref, yc_ref, yd_ref, bnw_ref, w_ref, o_ref, ob_ref, ro_ref):
    yb = _rms(yb_ref[...], bnw_ref[...]).astype(BF16)
    acc = x_ref[...]
    for g, y in enumerate((ya_ref[...], yb, yc_ref[...], yd_ref[...])):
        acc = acc + _dot(y, w_ref[g * GROUP_W:(g + 1) * GROUP_W, :])
    o_ref[...] = acc
    ob_ref[...] = acc.astype(BF16)
    ro_ref[...] = jnp.broadcast_to(_rinv(jnp.sum(acc * acc, axis=-1, keepdims=True), acc.shape[1]), ro_ref.shape)


def _outproj(x2, ya, yb, yc, yd, attn_norm_w, w, *, tm=512):
    t, d = x2.shape
    yspec = pl.BlockSpec((tm, GROUP_W), lambda i: (i, 0))
    rows = pl.BlockSpec((tm, d), lambda i: (i, 0))
    return pl.pallas_call(
        _outproj_kernel,
        grid=(t // tm,),
        in_specs=[rows, yspec, yspec, yspec, yspec,
                  pl.BlockSpec((1, GROUP_W), lambda i: (0, 0)), pl.BlockSpec((4 * GROUP_W, d), lambda i: (0, 0))],
        out_specs=[rows, rows, pl.BlockSpec((tm, LANES), lambda i: (i, 0))],
        out_shape=[jax.ShapeDtypeStruct((t, d), F32), jax.ShapeDtypeStruct((t, d), BF16),
                   jax.ShapeDtypeStruct((t, LANES), F32)],
        compiler_params=_cparams("parallel"),
        name="outproj",
    )(x2, ya, yb, yc, yd, attn_norm_w, w)


def _hgrn_tables(c=CHUNK, levels=CHUNK_LEVELS):
    t = np.arange(c)
    rows = []
    for lvl in range(levels):
        h = 1 << lvl
        r = (t // (2 * h)) * (2 * h) + h
        up = (t // h) % 2 == 1
        u = t[None, :]
        m = np.where(up[:, None], (u > r[:, None]) & (u <= t[:, None]), (u > t[:, None]) & (u <= r[:, None]))
        rows.append(m)
    rows.append(t[None, :] <= t[:, None])
    rows.append(t[None, :] > t[:, None])
    mstack = np.concatenate(rows, axis=0).astype(np.float32)
    x = t[:, None] ^ t[None, :]
    lv = np.where(t[None, :] < t[:, None], np.floor(np.log2(np.maximum(x, 1))).astype(np.int32), -1)
    return mstack, lv.astype(np.int32)


_HGRN_MSTACK, _HGRN_LEVEL = _hgrn_tables()


def _hgrn_kernel(q_ref, f_ref, i_ref, g_ref, lbl_ref, nw_ref, mst_ref, lv_ref, o_ref, st_ref, *, layer):
    c = CHUNK

    @pl.when(pl.program_id(1) == 0)
    def _():
        st_ref[...] = jnp.zeros_like(st_ref)

    lbl = lbl_ref[...]
    e = jnp.exp(lbl - jnp.max(lbl, axis=0, keepdims=True))
    p = e / jnp.sum(e, axis=0, keepdims=True)
    lb = jnp.zeros((1, GROUP_W), F32)
    for i in range(1, layer + 1):
        lb = lb + p[i:i + 1, :]

    lax.fori_loop(0, STEP_CHUNKS, functools.partial(
        _hgrn_chunk, q_ref, f_ref, i_ref, g_ref, nw_ref, mst_ref, lv_ref, o_ref, st_ref, lb), 0)


def _hgrn_chunk(q_ref, f_ref, i_ref, g_ref, nw_ref, mst_ref, lv_ref, o_ref, st_ref, lb, ci, carry):
    c = CHUNK
    rows = pl.ds(pl.multiple_of(ci * c, c), c)
    mst = mst_ref[...]
    lv = lv_ref[...]
    row = lax.broadcasted_iota(jnp.int32, (c, 1), 0)
    for h in range(HG_HEADS):
        sl = slice(h * HG_HEAD_W, (h + 1) * HG_HEAD_W)
        q = _silu(q_ref[h, rows, :])
        z = f_ref[h, rows, :]
        v = i_ref[h, rows, :]
        lbh = lb[:, sl]
        g = jnp.log(lbh + (1.0 - lbh) * jax.nn.sigmoid(z))
        k = (1.0 - lbh) * jax.nn.sigmoid(-z)
        ex = jnp.exp(_sum01_matmul2(mst, g))
        a = jnp.zeros((c, c), F32)
        for lvl in range(CHUNK_LEVELS):
            exl = ex[lvl * c:(lvl + 1) * c]
            up = ((row >> lvl) & 1) == 1
            qp = jnp.where(up, q * exl, 0.0).astype(BF16)
            kp = jnp.where(up, 0.0, k * exl).astype(BF16)
            a = jnp.where(lv == lvl, _dot_nt(qp, kp), a)
        vb = v.astype(BF16)
        ex_b = ex[CHUNK_LEVELS * c:(CHUNK_LEVELS + 1) * c]
        ex_r = ex[(CHUNK_LEVELS + 1) * c:(CHUNK_LEVELS + 2) * c]
        st = st_ref[h]
        o = (_dot(a.astype(BF16), vb) + jnp.sum(q * k, axis=-1, keepdims=True) * v
             + _dot_nt((q * ex_b).astype(BF16), st.astype(BF16)))
        st_ref[h] = ex_b[c - 1:c, :] * st + _dot_tn(vb, (k * ex_r).astype(BF16))
        o = _rms(o, nw_ref[:, sl]) * _silu(g_ref[h, rows, :])
        o_ref[rows, sl] = o.astype(o_ref.dtype)
    return carry


def _slab_spec(blk, rows, n=SLABS):
    first = blk * SLABS // n
    return pl.BlockSpec((n, None, rows, LANES), lambda bi, ci: (first, bi, ci, 0))


def _hgrn(proj4, lb_logits, norm_w, *, layer):
    _, b, s, _ = proj4.shape
    cspec = lambda blk: _slab_spec(blk, STEP_ROWS)
    full = lambda shape: pl.BlockSpec(shape, lambda bi, ci: (0,) * len(shape))
    mst = jnp.asarray(_HGRN_MSTACK, BF16)
    lv = jnp.asarray(_HGRN_LEVEL)
    return pl.pallas_call(
        functools.partial(_hgrn_kernel, layer=layer),
        grid=(b, s // STEP_ROWS),
        in_specs=[cspec(BLK_A), cspec(BLK_A + 1), cspec(BLK_A + 2), cspec(BLK_A + 3),
                  full(lb_logits.shape), full((1, GROUP_W)), full(mst.shape), full(lv.shape)],
        out_specs=pl.BlockSpec((None, STEP_ROWS, GROUP_W), lambda bi, ci: (bi, ci, 0)),
        out_shape=jax.ShapeDtypeStruct((b, s, GROUP_W), BF16),
        scratch_shapes=[pltpu.VMEM((HG_HEADS, HG_HEAD_W, HG_HEAD_W), F32)],
        compiler_params=_cparams("parallel", "arbitrary"),
        name="hgrn2",
    )(proj4, proj4, proj4, proj4, lb_logits, norm_w, mst, lv)


ATT_TILE = 2048
ATT_GROUP = 4
ATT_MID = 4
assert ATT_DILATIONS == (1, ATT_MID, ATT_MID * ATT_MID) and ATT_TILE == ATT_SPAN * ATT_DILATIONS[-1]


def _attend_group(qs, ks, vs, first_flags, bias_all, bias_first, first_head):
    blk = ATT_SPAN
    scores = []
    for q2, k2, ff in zip(qs, ks, first_flags):
        qq = jnp.concatenate([jnp.where(first_head, q2, 0.0), jnp.where(first_head, 0.0, q2)], axis=0).astype(BF16)
        bias = jnp.where(ff > 0, bias_first, bias_all)
        scores.append(_dot_nt(qq, k2) + jnp.concatenate([bias, bias], axis=0))
    s = jnp.concatenate(scores, axis=0)
    m = jnp.max(s, axis=-1, keepdims=True)
    p = jnp.exp(s - m)
    den = jnp.sum(p, axis=-1, keepdims=True)
    lse = m + jnp.log(den)
    pb = p.astype(BF16)
    outs = []
    for g, v2 in enumerate(vs):
        r0 = g * 2 * blk
        pv = _dot(pb[r0:r0 + 2 * blk], v2)
        d0, d1 = den[r0:r0 + blk], den[r0 + blk:r0 + 2 * blk]
        l0, l1 = lse[r0:r0 + blk], lse[r0 + blk:r0 + 2 * blk]
        o2 = jnp.where(first_head, pv[:blk], pv[blk:]) / jnp.where(first_head, d0, d1)
        outs.append((o2, jnp.where(first_head, l0, l1)))
    return outs


def _attn_kernel(q_ref, k_ref, v_ref, y_ref, q4_ref, k4_ref, v4_ref, kd1, vd1, kd4, vd4, kd16, vd16, op_ref, lp_ref):
    i = pl.program_id(2)
    blk, tile, mid = ATT_SPAN, ATT_TILE, ATT_MID
    len4 = tile // mid
    pitch4 = blk + len4
    pitch16 = 2 * blk
    scale = ATT_HEAD_DIM ** -0.5
    zeros = jnp.zeros((blk, LANES), BF16)

    @pl.when(i == 0)
    def _():
        for kd, vd, pitch, nres in ((kd1, vd1, 0, 1), (kd4, vd4, pitch4, mid), (kd16, vd16, pitch16, mid * mid)):
            for r in range(nres):
                kd[r * pitch:r * pitch + blk, :] = zeros
                vd[r * pitch:r * pitch + blk, :] = zeros

    kd1[blk:blk + tile, :] = k_ref[...].astype(BF16)
    vd1[blk:blk + tile, :] = v_ref[...].astype(BF16)
    for r in range(mid):
        rows = slice(r * len4, (r + 1) * len4)
        cur = slice(r * pitch4 + blk, (r + 1) * pitch4)
        kk = k_ref[pl.ds(r, len4, stride=mid), :]
        vv = v_ref[pl.ds(r, len4, stride=mid), :]
        k4_ref[rows, :] = kk
        v4_ref[rows, :] = vv
        kd4[cur, :] = kk.astype(BF16)
        vd4[cur, :] = vv.astype(BF16)
        q4_ref[rows, :] = q_ref[pl.ds(r, len4, stride=mid), :] * scale
    for r4 in range(mid):
        for hi in range(mid):
            r16 = r4 + mid * hi
            cur = slice(r16 * pitch16 + blk, (r16 + 1) * pitch16)
            kd16[cur, :] = k4_ref[pl.ds(r4 * len4 + hi, blk, stride=mid), :].astype(BF16)
            vd16[cur, :] = v4_ref[pl.ds(r4 * len4 + hi, blk, stride=mid), :].astype(BF16)

    iq = lax.broadcasted_iota(jnp.int32, (blk, 2 * blk), 0)
    ik = lax.broadcasted_iota(jnp.int32, (blk, 2 * blk), 1)
    dist = iq + blk - ik
    valid = (dist >= 0) & (dist <= blk)
    bias_all = jnp.where(valid, 0.0, MASK_VALUE)
    bias_first = jnp.where(valid & (ik >= blk), 0.0, MASK_VALUE)
    lane = lax.broadcasted_iota(jnp.int32, (1, LANES), 1)
    first_head = lane < ATT_HEAD_DIM
    tile0 = 1 - jnp.minimum(i, 1)
    grp = ATT_GROUP

    def attend(qs, kd, vd, bases, flags):
        ks = [kd[pl.ds(pl.multiple_of(b0, blk), 2 * blk), :] for b0 in bases]
        vs = [vd[pl.ds(pl.multiple_of(b0, blk), 2 * blk), :] for b0 in bases]
        return _attend_group(qs, ks, vs, flags, bias_all, bias_first, first_head)

    def pattern1(it, carry):
        b0s = [(it * grp + g) * blk for g in range(grp)]
        qs = [q_ref[pl.ds(pl.multiple_of(b0, blk), blk), :] * scale for b0 in b0s]
        flags = [tile0 * (1 - jnp.minimum(it * grp + g, 1)) for g in range(grp)]
        for b0, (o2, l2) in zip(b0s, attend(qs, kd1, vd1, b0s, flags)):
            op_ref[0, pl.ds(pl.multiple_of(b0, blk), blk), :] = o2
            lp_ref[0, pl.ds(pl.multiple_of(b0, blk), blk), :] = l2
        return carry

    def pattern4(r, carry):
        qs = [q4_ref[pl.ds(pl.multiple_of(r * len4 + g * blk, blk), blk), :] for g in range(grp)]
        bases = [r * pitch4 + g * blk for g in range(grp)]
        flags = [tile0 if g == 0 else 0 * tile0 for g in range(grp)]
        for g, (o2, l2) in enumerate(attend(qs, kd4, vd4, bases, flags)):
            op_ref[1, pl.ds(r + mid * blk * g, blk, stride=mid), :] = o2
            lp_ref[1, pl.ds(r + mid * blk * g, blk, stride=mid), :] = l2
        return carry

    def pattern16(hi, carry):
        qs = [q4_ref[pl.ds(g * len4 + hi, blk, stride=mid), :] for g in range(grp)]
        bases = [(g + mid * hi) * pitch16 for g in range(grp)]
        flags = [tile0] * grp
        for g, (o2, l2) in enumerate(attend(qs, kd16, vd16, bases, flags)):
            op_ref[2, pl.ds(g + mid * hi, blk, stride=mid * mid), :] = o2
            lp_ref[2, pl.ds(g + mid * hi, blk, stride=mid * mid), :] = l2
        return carry

    assert grp == mid and tile // blk == grp * mid
    lax.fori_loop(0, mid, pattern1, 0)
    lax.fori_loop(0, mid, pattern4, 0)
    lax.fori_loop(0, mid, pattern16, 0)

    for kd, vd, pitch, nres, ln in ((kd1, vd1, 0, 1, tile), (kd4, vd4, pitch4, mid, len4),
                                    (kd16, vd16, pitch16, mid * mid, blk)):
        for r in range(nres):
            kd[r * pitch:r * pitch + blk, :] = kd[r * pitch + ln:r * pitch + ln + blk, :]
            vd[r * pitch:r * pitch + blk, :] = vd[r * pitch + ln:r * pitch + ln + blk, :]

    def merge(c, carry):
        rows = pl.ds(pl.multiple_of(c * CHUNK, CHUNK), CHUNK)
        la, lb, lc = lp_ref[0, rows, :], lp_ref[1, rows, :], lp_ref[2, rows, :]
        m = jnp.maximum(jnp.maximum(la, lb), lc)
        wa, wb, wc = jnp.exp(la - m), jnp.exp(lb - m), jnp.exp(lc - m)
        y_ref[rows, :] = (wa * op_ref[0, rows, :] + wb * op_ref[1, rows, :] + wc * op_ref[2, rows, :]) / (wa + wb + wc)
        return carry

    lax.fori_loop(0, tile // CHUNK, merge, 0)


def _attn(proj4):
    _, b, s, _ = proj4.shape
    tile, blk, mid = ATT_TILE, ATT_SPAN, ATT_MID
    spec = lambda off: pl.BlockSpec((None, None, tile, LANES),
                                    lambda bi, pr, ti: ((BLK_B + off) * SLABS + pr, bi, ti, 0))
    return pl.pallas_call(
        _attn_kernel,
        grid=(b, SLABS, s // tile),
        in_specs=[spec(0), spec(1), spec(2)],
        out_specs=pl.BlockSpec((None, tile, LANES), lambda bi, pr, ti: (bi, ti, pr)),
        out_shape=jax.ShapeDtypeStruct((b, s, GROUP_W), F32),
        scratch_shapes=[pltpu.VMEM((tile, LANES), F32)] * 3
        + [pltpu.VMEM((blk + tile, LANES), BF16)] * 2
        + [pltpu.VMEM((mid * (blk + tile // mid), LANES), BF16)] * 2
        + [pltpu.VMEM((mid * mid * 2 * blk, LANES), BF16)] * 2
        + [pltpu.VMEM((len(ATT_DILATIONS), tile, LANES), F32)] * 2,
        compiler_params=_cparams("parallel", "parallel", "arbitrary"),
        name="dilattn",
    )(proj4, proj4, proj4)


def _causal_conv(x_ref, rows, xpad_ref, w_ref, b_ref):
    c = CHUNK
    for p in range(x_ref.shape[0]):
        xpad_ref[SUBLANES:SUBLANES + c, p * LANES:(p + 1) * LANES] = x_ref[p, rows, :]
    acc = b_ref[...] + w_ref[0:1, :] * xpad_ref[pl.ds(SUBLANES - CONV_W + 1, c), :]
    for j in range(1, CONV_W):
        acc = acc + w_ref[j:j + 1, :] * xpad_ref[pl.ds(SUBLANES - CONV_W + 1 + j, c), :]
    xpad_ref[0:SUBLANES, :] = xpad_ref[c:c + SUBLANES, :]
    return acc


def _pair_lanes(col_fn, first_head):
    return jnp.where(first_head, col_fn(0), col_fn(1))


def _ssd_kernel(z_ref, xbc_ref, dt_ref, cw_ref, cb_ref, dtb_ref, alog_ref, dsk_ref, nw_ref, tril_ref,
                y_ref, xpad_ref, st_ref):
    @pl.when(pl.program_id(1) == 0)
    def _():
        st_ref[...] = jnp.zeros_like(st_ref)
        xpad_ref[0:SUBLANES, :] = jnp.zeros((SUBLANES, xpad_ref.shape[1]), F32)

    lax.fori_loop(0, STEP_CHUNKS, functools.partial(
        _ssd_chunk, z_ref, xbc_ref, dt_ref, cw_ref, cb_ref, dtb_ref, alog_ref, dsk_ref, nw_ref, tril_ref,
        y_ref, xpad_ref, st_ref), 0)


def _ssd_chunk(z_ref, xbc_ref, dt_ref, cw_ref, cb_ref, dtb_ref, alog_ref, dsk_ref, nw_ref, tril_ref,
               y_ref, xpad_ref, st_ref, ci, carry):
    c = CHUNK
    rows = pl.ds(pl.multiple_of(ci * c, c), c)
    xbc = _silu(_causal_conv(xbc_ref, rows, xpad_ref, cw_ref, cb_ref))
    dt = _softplus(dt_ref[rows, :] + dtb_ref[...])
    adt = dt * (-jnp.exp(alog_ref[...]))
    tril01 = tril_ref[...]
    acs = _sum01_matmul(tril01, adt)
    acs_t = acs.T
    e_cs = jnp.exp(acs)
    dt_rest = dt * jnp.exp(acs[c - 1:c, :] - acs)
    causal = tril01 > 0

    lane = lax.broadcasted_iota(jnp.int32, (1, LANES), 1)
    first_head = lane < SSM_HEAD_DIM
    gw = GROUP_W // SSM_GROUPS
    hpg = SSM_HEADS // SSM_GROUPS
    for g in range(SSM_GROUPS):
        bg = xbc[:, GROUP_W + g * SSM_STATE:GROUP_W + (g + 1) * SSM_STATE]
        cg = xbc[:, GROUP_W + (SSM_GROUPS + g) * SSM_STATE:GROUP_W + (SSM_GROUPS + g + 1) * SSM_STATE].astype(BF16)
        cb = jnp.where(causal, _dot_nt(cg, bg.astype(BF16)), 0.0)
        st = st_ref[g]
        y_off = _dot(cg, st.astype(BF16))
        ys, xds, dec_last = [], [], []
        for pr in range(hpg // 2):
            h0 = g * hpg + 2 * pr
            ls = slice(h0 * SSM_HEAD_DIM, (h0 + 2) * SSM_HEAD_DIM)
            xs = xbc[:, ls]
            xdt = (xs * _pair_lanes(lambda i: dt[:, h0 + i:h0 + i + 1], first_head)).astype(BF16)
            yd = []
            for i in range(2):
                hh = h0 + i
                seg = jnp.minimum(acs[:, hh:hh + 1] - acs_t[hh:hh + 1, :], 0.0)
                yd.append(_dot((cb * jnp.exp(seg)).astype(BF16), xdt))
            y_pair = (jnp.where(first_head, yd[0], yd[1])
                      + y_off[:, pr * LANES:(pr + 1) * LANES] * _pair_lanes(lambda i: e_cs[:, h0 + i:h0 + i + 1], first_head)
                      + dsk_ref[:, ls] * xs)
            ys.append(y_pair)
            xds.append((xs * _pair_lanes(lambda i: dt_rest[:, h0 + i:h0 + i + 1], first_head)).astype(BF16))
            dec_last.append(_pair_lanes(lambda i: e_cs[c - 1:c, h0 + i:h0 + i + 1], first_head))
        st_ref[g] = (jnp.concatenate(dec_last, axis=1) * st
                     + _dot_tn(bg.astype(BF16), jnp.concatenate(xds, axis=1)))
        zg = jnp.concatenate([z_ref[g * (gw // LANES) + p, rows, :] for p in range(gw // LANES)], axis=1)
        yg = jnp.concatenate(ys, axis=1) * _silu(zg)
        y_ref[rows, g * gw:(g + 1) * gw] = _rms(yg, nw_ref[:, g * gw:(g + 1) * gw]).astype(y_ref.dtype)
    return carry


def _ssd(proj4, conv_w, conv_b, dt_bias, a_log, d_skip, norm_w):
    _, b, s, _ = proj4.shape
    full = lambda shape: pl.BlockSpec(shape, lambda bi, ci: (0,) * len(shape))
    tril = jnp.asarray(np.tril(np.ones((CHUNK, CHUNK), np.float32)), BF16)
    return pl.pallas_call(
        _ssd_kernel,
        grid=(b, s // STEP_ROWS),
        in_specs=[
            _slab_spec(BLK_CZ, STEP_ROWS),
            _slab_spec(BLK_CXBC, STEP_ROWS, n=SSM_CONV_DIM // LANES),
            pl.BlockSpec((None, None, STEP_ROWS, LANES), lambda bi, ci: (BLK_CDT * SLABS, bi, ci, 0)),
            full((CONV_W, SSM_CONV_DIM)), full((1, SSM_CONV_DIM)), full((1, LANES)), full((1, LANES)),
            full((1, GROUP_W)), full((1, GROUP_W)), full((CHUNK, CHUNK)),
        ],
        out_specs=pl.BlockSpec((None, STEP_ROWS, GROUP_W), lambda bi, ci: (bi, ci, 0)),
        out_shape=jax.ShapeDtypeStruct((b, s, GROUP_W), BF16),
        scratch_shapes=[pltpu.VMEM((CHUNK + SUBLANES, SSM_CONV_DIM), F32),
                        pltpu.VMEM((SSM_GROUPS, SSM_STATE, GROUP_W // SSM_GROUPS), F32)],
        compiler_params=_cparams("parallel", "arbitrary"),
        name="ssd",
    )(proj4, proj4, proj4, conv_w, conv_b, dt_bias, a_log, d_skip, norm_w, tril)


def _gelu_tanh(x):
    return 0.5 * x * (1.0 + jnp.tanh(np.sqrt(2.0 / np.pi).astype(np.float32) * (x + 0.044715 * (x * x * x))))


def _lru_kernel(x_ref, g_ref, cw_ref, cb_ref, wa_ref, ba_ref, wx_ref, bx_ref, ap_ref, nw_ref,
                y_ref, xpad_ref, h_ref):
    @pl.when(pl.program_id(1) == 0)
    def _():
        h_ref[...] = jnp.zeros_like(h_ref)
        xpad_ref[0:SUBLANES, :] = jnp.zeros((SUBLANES, xpad_ref.shape[1]), F32)

    lax.fori_loop(0, STEP_CHUNKS, functools.partial(
        _lru_chunk, x_ref, g_ref, cw_ref, cb_ref, wa_ref, ba_ref, wx_ref, bx_ref, ap_ref, nw_ref,
        y_ref, xpad_ref, h_ref), 0)


def _lru_chunk(x_ref, g_ref, cw_ref, cb_ref, wa_ref, ba_ref, wx_ref, bx_ref, ap_ref, nw_ref,
               y_ref, xpad_ref, h_ref, ci, carry):
    c = CHUNK
    rows = pl.ds(pl.multiple_of(ci * c, c), c)
    xc = _causal_conv(x_ref, rows, xpad_ref, cw_ref, cb_ref)
    xb = xc.astype(BF16)
    r = jax.nn.sigmoid(_dot(xb, wa_ref[...]) + ba_ref[...])
    i = jax.nn.sigmoid(_dot(xb, wx_ref[...]) + bx_ref[...])
    log_a = -LRU_C * r * _softplus(-ap_ref[...])
    a = jnp.exp(log_a)
    bt = jnp.sqrt(jnp.maximum(-_expm1(2.0 * log_a), 0.0)) * (i * xc)
    row = lax.broadcasted_iota(jnp.int32, (c, 1), 0)
    d = 1
    while d < c:
        keep = row >= d
        a_s = pltpu.roll(a, d, 0)
        b_s = pltpu.roll(bt, d, 0)
        bt = jnp.where(keep, a * b_s + bt, bt)
        a = jnp.where(keep, a * a_s, a)
        d *= 2
    h = bt + a * h_ref[...]
    h_ref[...] = h[c - 1:c, :]
    gate = jnp.concatenate([g_ref[p, rows, :] for p in range(SLABS)], axis=1)
    y_ref[rows, :] = _rms(h * _gelu_tanh(gate), nw_ref[...]).astype(y_ref.dtype)
    return carry


def _lru(proj4, conv_w, conv_b, wa_bd, b_a, wx_bd, b_x, a_param, norm_w):
    _, b, s, _ = proj4.shape
    full = lambda shape: pl.BlockSpec(shape, lambda bi, ci: (0,) * len(shape))
    vec = full((1, GROUP_W))
    return pl.pallas_call(
        _lru_kernel,
        grid=(b, s // STEP_ROWS),
        in_specs=[
            _slab_spec(BLK_DX, STEP_ROWS),
            _slab_spec(BLK_DG, STEP_ROWS),
            full((CONV_W, GROUP_W)), vec, full((GROUP_W, GROUP_W)), vec, full((GROUP_W, GROUP_W)), vec, vec, vec,
        ],
        out_specs=pl.BlockSpec((None, STEP_ROWS, GROUP_W), lambda bi, ci: (bi, ci, 0)),
        out_shape=jax.ShapeDtypeStruct((b, s, GROUP_W), BF16),
        scratch_shapes=[pltpu.VMEM((CHUNK + SUBLANES, GROUP_W), F32), pltpu.VMEM((1, GROUP_W), F32)],
        compiler_params=_cparams("parallel", "arbitrary"),
        name="rglru",
    )(proj4, proj4, conv_w, conv_b, wa_bd, b_a, wx_bd, b_x, a_param, norm_w)


def _block_diag(w):
    nb, n, _ = w.shape
    eye = jnp.eye(nb, dtype=w.dtype)
    return jnp.einsum("hij,hg->higj", w, eye).reshape(nb * n, nb * n)


def _pad_lanes(v, width=LANES):
    return jnp.pad(v, (0, width - v.shape[0])).reshape(1, width)


def kernel(x, ffn1_norm, ffn1_w_gate, ffn1_w_up, ffn1_w_down, mix_norm, w_in, w_out, hgrn_lb_logits, hgrn_norm, attn_norm, ssm_conv_w, ssm_conv_b, ssm_dt_bias, ssm_a_log, ssm_d, ssm_norm, lru_conv_w, lru_conv_b, lru_w_a, lru_b_a, lru_w_x, lru_b_x, lru_a_param, lru_norm, ffn2_norm, ffn2_w_gate, ffn2_w_up, ffn2_w_down, final_norm):
    bsz, seq, d = x.shape
    depth = w_in.shape[0]
    t = bsz * seq
    row = lambda v: v.reshape(1, -1).astype(F32)
    x2 = x.reshape(t, d)
    lb_logits = hgrn_lb_logits.astype(F32)
    ffn1_w = (_tile_cols(ffn1_w_gate, FFN_TF, ffn1_norm), _tile_cols(ffn1_w_up, FFN_TF, ffn1_norm),
              _tile_cols(ffn1_w_down, FFN_TN))
    ffn2_w = (_tile_cols(ffn2_w_gate, FFN_TF, ffn2_norm), _tile_cols(ffn2_w_up, FFN_TF, ffn2_norm),
              _tile_cols(ffn2_w_down, FFN_TN))
    w_in_t = _tile_cols(jnp.concatenate(
        [w_in[:, :, :N_IN_BEFORE_PAD],
         jnp.zeros((depth, d, (BLK_DX * GROUP_W) - N_IN_BEFORE_PAD), w_in.dtype),
         w_in[:, :, N_IN_BEFORE_PAD:]], axis=2), INPROJ_TN, mix_norm)
    stream = (x2, *_stream_prep(x2))
    for l in range(depth):
        stream = _ffn(stream, *ffn1_w, l)
        x2 = stream[0]
        proj4 = _inproj(stream, w_in_t, l).reshape(PROJ_BLOCKS * SLABS, bsz, seq, LANES)
        y_a = _hgrn(proj4, lb_logits, row(hgrn_norm[l]), layer=l).reshape(t, GROUP_W)
        y_b = _attn(proj4).reshape(t, GROUP_W)
        y_c = _ssd(proj4, ssm_conv_w[l].astype(F32), row(ssm_conv_b[l]), _pad_lanes(ssm_dt_bias[l].astype(F32)),
                   _pad_lanes(ssm_a_log[l].astype(F32)), row(jnp.repeat(ssm_d[l], SSM_HEAD_DIM)),
                   row(ssm_norm[l])).reshape(t, GROUP_W)
        y_d = _lru(proj4, lru_conv_w[l].astype(F32), row(lru_conv_b[l]), _block_diag(lru_w_a[l]).astype(BF16),
                   row(lru_b_a[l]), _block_diag(lru_w_x[l]).astype(BF16), row(lru_b_x[l]), row(lru_a_param[l]),
                   row(lru_norm[l])).reshape(t, GROUP_W)
        stream = _outproj(x2, y_a, y_b, y_c, y_d, row(attn_norm[l]), w_out[l].astype(BF16))
        stream = _ffn(stream, *ffn2_w, l)
    return _final_norm(stream[0], row(final_norm)).reshape(bsz, seq, d)
```

```python
import functools

import numpy as np
import jax
import jax.numpy as jnp
from jax import lax
from jax.experimental import pallas as pl
from jax.experimental.pallas import tpu as pltpu

F32 = jnp.float32
BF16 = jnp.bfloat16

NORM_EPS = 1e-6
MASK_VALUE = -1e30
GROUP_W = 512
HG_HEADS = 4
HG_HEAD_W = GROUP_W // HG_HEADS
ATT_HEAD_DIM = 64
ATT_SPAN = 128
ATT_DILATIONS = (1, 4, 16)
SSM_HEADS = 8
SSM_HEAD_DIM = 64
SSM_GROUPS = 2
SSM_STATE = 128
SSM_CONV_DIM = GROUP_W + 2 * SSM_GROUPS * SSM_STATE
CONV_W = 4
LRU_C = 8.0

LANES = 128
SUBLANES = 8
VMEM_LIMIT = 60 * 1024 * 1024

PROJ_BLOCKS = 13
PROJ_W = PROJ_BLOCKS * GROUP_W
BLK_A = 0
BLK_B = 4
BLK_CZ = 7
BLK_CXBC = 8
BLK_CDT = 10
BLK_DX = 11
BLK_DG = 12
N_IN_BEFORE_PAD = 10 * GROUP_W + SSM_HEADS
SLABS = GROUP_W // LANES

CHUNK = 256
CHUNK_LEVELS = 8
STEP_CHUNKS = 2
STEP_ROWS = STEP_CHUNKS * CHUNK


def _cparams(*sem):
    return pltpu.CompilerParams(dimension_semantics=sem, vmem_limit_bytes=VMEM_LIMIT)


def _rms(x, w):
    return x * lax.rsqrt(jnp.mean(x * x, axis=-1, keepdims=True) + NORM_EPS) * w


def _silu(x):
    return x * jax.nn.sigmoid(x)


def _softplus(x):
    return jnp.maximum(x, 0.0) + jnp.log1p(jnp.exp(-jnp.abs(x)))


def _expm1(u):
    w = jnp.exp(u)
    near = (jnp.abs(u) < 0.5) & (w != 1.0)
    kahan = (w - 1.0) * u / jnp.log(jnp.where(near, w, 2.0))
    return jnp.where(near, kahan, jnp.where(w == 1.0, u, w - 1.0))


def _dot(a, b):
    return jnp.dot(a, b, preferred_element_type=F32)


def _dot_nt(a, b):
    return lax.dot_general(a, b, (((1,), (1,)), ((), ())), preferred_element_type=F32)


def _dot_tn(a, b):
    return lax.dot_general(a, b, (((0,), (0,)), ((), ())), preferred_element_type=F32)


def _sum01_matmul(m01, x):
    hi = x.astype(BF16)
    r1 = x - hi.astype(F32)
    mid = r1.astype(BF16)
    lo = (r1 - mid.astype(F32)).astype(BF16)
    return _dot(m01, hi) + _dot(m01, mid) + _dot(m01, lo)


def _sum01_matmul2(m01, x):
    hi = x.astype(BF16)
    mid = (x - hi.astype(F32)).astype(BF16)
    both = _dot(m01, jnp.concatenate([hi, mid], axis=1))
    n = x.shape[1]
    return both[:, :n] + both[:, n:]


def _row_scale(r, width):
    return jnp.concatenate([r] * (width // LANES), axis=1)


def _rinv(ssq, d):
    return lax.rsqrt(ssq * (1.0 / d) + NORM_EPS)


def _stream_prep_kernel(x_ref, xb_ref, r_ref):
    x = x_ref[...]
    xb_ref[...] = x.astype(BF16)
    r_ref[...] = jnp.broadcast_to(_rinv(jnp.sum(x * x, axis=-1, keepdims=True), x.shape[1]), r_ref.shape)


def _stream_prep(x2, *, tm=1024):
    t, d = x2.shape
    return pl.pallas_call(
        _stream_prep_kernel,
        grid=(t // tm,),
        in_specs=[pl.BlockSpec((tm, d), lambda i: (i, 0))],
        out_specs=[pl.BlockSpec((tm, d), lambda i: (i, 0)), pl.BlockSpec((tm, LANES), lambda i: (i, 0))],
        out_shape=[jax.ShapeDtypeStruct((t, d), BF16), jax.ShapeDtypeStruct((t, LANES), F32)],
        compiler_params=_cparams("parallel"),
        name="stream_prep",
    )(x2)


def _ffn_kernel(xb_ref, r_ref, xc_ref, wg_ref, wu_ref, wd_ref, o_ref, ob_ref, ro_ref, a_ref, ssq_ref,
                *, n_f, tf, d):
    j = pl.program_id(1)

    @pl.when(j < n_f)
    def _():
        xb = xb_ref[...]
        rt = _row_scale(r_ref[...], tf)
        a_ref[j] = (_silu(_dot(xb, wg_ref[...]) * rt) * (_dot(xb, wu_ref[...]) * rt)).astype(BF16)

    @pl.when(j >= n_f)
    def _():
        acc = _dot(a_ref[0], wd_ref[0:tf, :])
        for jj in range(1, n_f):
            acc = acc + _dot(a_ref[jj], wd_ref[jj * tf:(jj + 1) * tf, :])
        y = xc_ref[...] + 0.5 * acc
        o_ref[...] = y
        ob_ref[...] = y.astype(BF16)
        part = jnp.broadcast_to(jnp.sum(y * y, axis=-1, keepdims=True), ssq_ref.shape)
        ssq = part + jnp.where(j == n_f, 0.0, ssq_ref[...])
        ssq_ref[...] = ssq
        ro_ref[...] = _rinv(ssq, d)


FFN_TM = 1024
FFN_TF = 512
FFN_TN = 512


def _tile_cols_kernel(w_ref, g_ref, o_ref):
    o_ref[...] = (w_ref[...] * g_ref[...]).astype(BF16)


def _tile_cols(w, tile, gain=None):
    nl, k, n = w.shape
    kb = k // 2
    g = jnp.ones((nl, k), F32) if gain is None else gain.astype(F32)
    return pl.pallas_call(
        _tile_cols_kernel,
        grid=(nl, n // tile, k // kb),
        in_specs=[pl.BlockSpec((None, kb, tile), lambda l, j, kk: (l, kk, j)),
                  pl.BlockSpec((None, kb, 1), lambda l, j, kk: (l, kk, 0))],
        out_specs=pl.BlockSpec((None, None, kb, tile), lambda l, j, kk: (l, j, kk, 0)),
        out_shape=jax.ShapeDtypeStruct((nl, n // tile, k, tile), BF16),
        compiler_params=_cparams("parallel", "parallel", "parallel"),
        name="tile_cols",
    )(w.astype(F32), g.reshape(nl, k, 1))


def _ffn(stream, wg_t, wu_t, wd_t, layer, *, tm=FFN_TM):
    x2, xb, r = stream
    t, d = x2.shape
    _, n_f, _, tf = wg_t.shape
    _, n_d, f, tn = wd_t.shape
    up = lambda i, j: (layer, jnp.minimum(j, n_f - 1), 0, 0)
    down = lambda i, j: (layer, jnp.maximum(j - n_f, 0), 0, 0)
    col = lambda i, j: (i, jnp.maximum(j - n_f, 0))
    rows = lambda i, j: (i, 0)
    return pl.pallas_call(
        functools.partial(_ffn_kernel, n_f=n_f, tf=tf, d=d),
        grid=(t // tm, n_f + n_d),
        in_specs=[
            pl.BlockSpec((tm, d), rows),
            pl.BlockSpec((tm, LANES), rows),
            pl.BlockSpec((tm, tn), col),
            pl.BlockSpec((None, None, d, tf), up),
            pl.BlockSpec((None, None, d, tf), up),
            pl.BlockSpec((None, None, f, tn), down),
        ],
        out_specs=[pl.BlockSpec((tm, tn), col), pl.BlockSpec((tm, tn), col), pl.BlockSpec((tm, LANES), rows)],
        out_shape=[jax.ShapeDtypeStruct((t, d), F32), jax.ShapeDtypeStruct((t, d), BF16),
                   jax.ShapeDtypeStruct((t, LANES), F32)],
        scratch_shapes=[pltpu.VMEM((n_f, tm, tf), BF16), pltpu.VMEM((tm, LANES), F32)],
        compiler_params=_cparams("parallel", "arbitrary"),
        name="ffn",
    )(xb, r, x2, wg_t, wu_t, wd_t)


def _final_norm_kernel(x_ref, w_ref, o_ref):
    o_ref[...] = _rms(x_ref[...], w_ref[...])


def _final_norm(x2, w, *, tm=1024):
    t, d = x2.shape
    return pl.pallas_call(
        _final_norm_kernel,
        grid=(t // tm,),
        in_specs=[pl.BlockSpec((tm, d), lambda i: (i, 0)), pl.BlockSpec((1, d), lambda i: (0, 0))],
        out_specs=pl.BlockSpec((tm, d), lambda i: (i, 0)),
        out_shape=jax.ShapeDtypeStruct((t, d), F32),
        compiler_params=_cparams("parallel"),
        name="final_norm",
    )(x2, w)


def _inproj_kernel(xb_ref, r_ref, w_ref, o_ref):
    r = r_ref[...]
    res = _dot(xb_ref[...], w_ref[...])
    for p in range(o_ref.shape[0]):
        o_ref[p] = res[:, p * LANES:(p + 1) * LANES] * r


INPROJ_TN = 2304
INPROJ_W = 3 * INPROJ_TN
assert INPROJ_W >= PROJ_W and INPROJ_TN % LANES == 0


def _inproj(stream, w_t, layer, *, tm=1024):
    _, xb, r = stream
    t, d = xb.shape
    _, n_t, _, tn = w_t.shape
    return pl.pallas_call(
        _inproj_kernel,
        grid=(t // tm, n_t),
        in_specs=[
            pl.BlockSpec((tm, d), lambda i, j: (i, 0)),
            pl.BlockSpec((tm, LANES), lambda i, j: (i, 0)),
            pl.BlockSpec((None, None, d, tn), lambda i, j: (layer, j, 0, 0)),
        ],
        out_specs=pl.BlockSpec((tn // LANES, tm, LANES), lambda i, j: (j, i, 0)),
        out_shape=jax.ShapeDtypeStruct((n_t * tn // LANES, t, LANES), F32),
        compiler_params=_cparams("parallel", "arbitrary"),
        name="inproj",
    )(xb, r, w_t)


def _outproj_kernel(x_ref, ya_ref, yb_ref, yc_ref, yd_ref, bnw_ref, w_ref, o_ref, ob_ref, ro_ref):
    yb = _rms(yb_ref[...], bnw_ref[...]).astype(BF16)
    acc = x_ref[...]
    for g, y in enumerate((ya_ref[...], yb, yc_ref[...], yd_ref[...])):
        acc = acc + _dot(y, w_ref[g * GROUP_W:(g + 1) * GROUP_W, :])
    o_ref[...] = acc
    ob_ref[...] = acc.astype(BF16)
    ro_ref[...] = jnp.broadcast_to(_rinv(jnp.sum(acc * acc, axis=-1, keepdims=True), acc.shape[1]), ro_ref.shape)


def _outproj(x2, ya, yb, yc, yd, attn_norm_w, w, *, tm=512):
    t, d = x2.shape
    yspec = pl.BlockSpec((tm, GROUP_W), lambda i: (i, 0))
    rows = pl.BlockSpec((tm, d), lambda i: (i, 0))
    return pl.pallas_call(
        _outproj_kernel,
        grid=(t // tm,),
        in_specs=[rows, yspec, yspec, yspec, yspec,
                  pl.BlockSpec((1, GROUP_W), lambda i: (0, 0)), pl.BlockSpec((4 * GROUP_W, d), lambda i: (0, 0))],
        out_specs=[rows, rows, pl.BlockSpec((tm, LANES), lambda i: (i, 0))],
        out_shape=[jax.ShapeDtypeStruct((t, d), F32), jax.ShapeDtypeStruct((t, d), BF16),
                   jax.ShapeDtypeStruct((t, LANES), F32)],
        compiler_params=_cparams("parallel"),
        name="outproj",
    )(x2, ya, yb, yc, yd, attn_norm_w, w)


def _hgrn_tables(c=CHUNK, levels=CHUNK_LEVELS):
    t = np.arange(c)
    rows = []
    for lvl in range(levels):
        h = 1 << lvl
        r = (t // (2 * h)) * (2 * h) + h
        up = (t // h) % 2 == 1
        u = t[None, :]
        m = np.where(up[:, None], (u > r[:, None]) & (u <= t[:, None]), (u > t[:, None]) & (u <= r[:, None]))
        rows.append(m)
    rows.append(t[None, :] <= t[:, None])
    rows.append(t[None, :] > t[:, None])
    mstack = np.concatenate(rows, axis=0).astype(np.float32)
    x = t[:, None] ^ t[None, :]
    lv = np.where(t[None, :] < t[:, None], np.floor(np.log2(np.maximum(x, 1))).astype(np.int32), -1)
    return mstack, lv.astype(np.int32)


_HGRN_MSTACK, _HGRN_LEVEL = _hgrn_tables()


def _hgrn_kernel(q_ref, f_ref, i_ref, g_ref, lbl_ref, nw_ref, mst_ref, lv_ref, o_ref, st_ref, *, layer):
    c = CHUNK

    @pl.when(pl.program_id(1) == 0)
    def _():
        st_ref[...] = jnp.zeros_like(st_ref)

    lbl = lbl_ref[...]
    e = jnp.exp(lbl - jnp.max(lbl, axis=0, keepdims=True))
    p = e / jnp.sum(e, axis=0, keepdims=True)
    lb = jnp.zeros((1, GROUP_W), F32)
    for i in range(1, layer + 1):
        lb = lb + p[i:i + 1, :]

    lax.fori_loop(0, STEP_CHUNKS, functools.partial(
        _hgrn_chunk, q_ref, f_ref, i_ref, g_ref, nw_ref, mst_ref, lv_ref, o_ref, st_ref, lb), 0)


def _hgrn_chunk(q_ref, f_ref, i_ref, g_ref, nw_ref, mst_ref, lv_ref, o_ref, st_ref, lb, ci, carry):
    c = CHUNK
    rows = pl.ds(pl.multiple_of(ci * c, c), c)
    mst = mst_ref[...]
    lv = lv_ref[...]
    row = lax.broadcasted_iota(jnp.int32, (c, 1), 0)
    for h in range(HG_HEADS):
        sl = slice(h * HG_HEAD_W, (h + 1) * HG_HEAD_W)
        q = _silu(q_ref[h, rows, :])
        z = f_ref[h, rows, :]
        v = i_ref[h, rows, :]
        lbh = lb[:, sl]
        g = jnp.log(lbh + (1.0 - lbh) * jax.nn.sigmoid(z))
        k = (1.0 - lbh) * jax.nn.sigmoid(-z)
        ex = jnp.exp(_sum01_matmul2(mst, g))
        a = jnp.zeros((c, c), F32)
        for lvl in range(CHUNK_LEVELS):
            exl = ex[lvl * c:(lvl + 1) * c]
            up = ((row >> lvl) & 1) == 1
            qp = jnp.where(up, q * exl, 0.0).astype(BF16)
            kp = jnp.where(up, 0.0, k * exl).astype(BF16)
            a = jnp.where(lv == lvl, _dot_nt(qp, kp), a)
        vb = v.astype(BF16)
        ex_b = ex[CHUNK_LEVELS * c:(CHUNK_LEVELS + 1) * c]
        ex_r = ex[(CHUNK_LEVELS + 1) * c:(CHUNK_LEVELS + 2) * c]
        st = st_ref[h]
        o = (_dot(a.astype(BF16), vb) + jnp.sum(q * k, axis=-1, keepdims=True) * v
             + _dot_nt((q * ex_b).astype(BF16), st.astype(BF16)))
        st_ref[h] = ex_b[c - 1:c, :] * st + _dot_tn(vb, (k * ex_r).astype(BF16))
        o = _rms(o, nw_ref[:, sl]) * _silu(g_ref[h, rows, :])
        o_ref[rows, sl] = o.astype(o_ref.dtype)
    return carry


def _slab_spec(blk, rows, n=SLABS):
    first = blk * SLABS // n
    return pl.BlockSpec((n, None, rows, LANES), lambda bi, ci: (first, bi, ci, 0))


def _hgrn(proj4, lb_logits, norm_w, *, layer):
    _, b, s, _ = proj4.shape
    cspec = lambda blk: _slab_spec(blk, STEP_ROWS)
    full = lambda shape: pl.BlockSpec(shape, lambda bi, ci: (0,) * len(shape))
    mst = jnp.asarray(_HGRN_MSTACK, BF16)
    lv = jnp.asarray(_HGRN_LEVEL)
    return pl.pallas_call(
        functools.partial(_hgrn_kernel, layer=layer),
        grid=(b, s // STEP_ROWS),
        in_specs=[cspec(BLK_A), cspec(BLK_A + 1), cspec(BLK_A + 2), cspec(BLK_A + 3),
                  full(lb_logits.shape), full((1, GROUP_W)), full(mst.shape), full(lv.shape)],
        out_specs=pl.BlockSpec((None, STEP_ROWS, GROUP_W), lambda bi, ci: (bi, ci, 0)),
        out_shape=jax.ShapeDtypeStruct((b, s, GROUP_W), BF16),
        scratch_shapes=[pltpu.VMEM((HG_HEADS, HG_HEAD_W, HG_HEAD_W), F32)],
        compiler_params=_cparams("parallel", "arbitrary"),
        name="hgrn2",
    )(proj4, proj4, proj4, proj4, lb_logits, norm_w, mst, lv)


ATT_TILE = 2048
ATT_GROUP = 8
ATT_MID = 4
assert ATT_DILATIONS == (1, ATT_MID, ATT_MID * ATT_MID) and ATT_TILE == ATT_SPAN * ATT_DILATIONS[-1]


def _attend_group(qs, ks, vs, first_flags, bias_all, bias_first, first_head):
    blk = ATT_SPAN
    scores = []
    for q2, k2, ff in zip(qs, ks, first_flags):
        qq = jnp.concatenate([jnp.where(first_head, q2, 0.0), jnp.where(first_head, 0.0, q2)], axis=0).astype(BF16)
        bias = jnp.where(ff > 0, bias_first, bias_all)
        scores.append(_dot_nt(qq, k2) + jnp.concatenate([bias, bias], axis=0))
    s = jnp.concatenate(scores, axis=0)
    m = jnp.max(s, axis=-1, keepdims=True)
    p = jnp.exp(s - m)
    den = jnp.sum(p, axis=-1, keepdims=True)
    lse = m + jnp.log(den)
    pb = p.astype(BF16)
    outs = []
    for g, v2 in enumerate(vs):
        r0 = g * 2 * blk
        pv = _dot(pb[r0:r0 + 2 * blk], v2)
        d0, d1 = den[r0:r0 + blk], den[r0 + blk:r0 + 2 * blk]
        l0, l1 = lse[r0:r0 + blk], lse[r0 + blk:r0 + 2 * blk]
        o2 = jnp.where(first_head, pv[:blk], pv[blk:]) / jnp.where(first_head, d0, d1)
        outs.append((o2, jnp.where(first_head, l0, l1)))
    return outs


def _attn_kernel(q_ref, k_ref, v_ref, y_ref, q4_ref, k4_ref, v4_ref, kd1, vd1, kd4, vd4, kd16, vd16, op_ref, lp_ref):
    i = pl.program_id(2)
    blk, tile, mid = ATT_SPAN, ATT_TILE, ATT_MID
    len4 = tile // mid
    pitch4 = blk + len4
    pitch16 = 2 * blk
    scale = ATT_HEAD_DIM ** -0.5
    zeros = jnp.zeros((blk, LANES), BF16)

    @pl.when(i == 0)
    def _():
        for kd, vd, pitch, nres in ((kd1, vd1, 0, 1), (kd4, vd4, pitch4, mid), (kd16, vd16, pitch16, mid * mid)):
            for r in range(nres):
                kd[r * pitch:r * pitch + blk, :] = zeros
                vd[r * pitch:r * pitch + blk, :] = zeros

    kd1[blk:blk + tile, :] = k_ref[...].astype(BF16)
    vd1[blk:blk + tile, :] = v_ref[...].astype(BF16)
    for r in range(mid):
        rows = slice(r * len4, (r + 1) * len4)
        cur = slice(r * pitch4 + blk, (r + 1) * pitch4)
        kk = k_ref[pl.ds(r, len4, stride=mid), :]
        vv = v_ref[pl.ds(r, len4, stride=mid), :]
        k4_ref[rows, :] = kk
        v4_ref[rows, :] = vv
        kd4[cur, :] = kk.astype(BF16)
        vd4[cur, :] = vv.astype(BF16)
        q4_ref[rows, :] = q_ref[pl.ds(r, len4, stride=mid), :] * scale
    for r4 in range(mid):
        for hi in range(mid):
            r16 = r4 + mid * hi
            cur = slice(r16 * pitch16 + blk, (r16 + 1) * pitch16)
            kd16[cur, :] = k4_ref[pl.ds(r4 * len4 + hi, blk, stride=mid), :].astype(BF16)
            vd16[cur, :] = v4_ref[pl.ds(r4 * len4 + hi, blk, stride=mid), :].astype(BF16)

    iq = lax.broadcasted_iota(jnp.int32, (blk, 2 * blk), 0)
    ik = lax.broadcasted_iota(jnp.int32, (blk, 2 * blk), 1)
    dist = iq + blk - ik
    valid = (dist >= 0) & (dist <= blk)
    bias_all = jnp.where(valid, 0.0, MASK_VALUE)
    bias_first = jnp.where(valid & (ik >= blk), 0.0, MASK_VALUE)
    lane = lax.broadcasted_iota(jnp.int32, (1, LANES), 1)
    first_head = lane < ATT_HEAD_DIM
    tile0 = 1 - jnp.minimum(i, 1)
    grp = ATT_GROUP

    def attend(qs, kd, vd, bases, flags):
        ks = [kd[pl.ds(pl.multiple_of(b0, blk), 2 * blk), :] for b0 in bases]
        vs = [vd[pl.ds(pl.multiple_of(b0, blk), 2 * blk), :] for b0 in bases]
        return _attend_group(qs, ks, vs, flags, bias_all, bias_first, first_head)

    def pattern1(it, carry):
        b0s = [(it * grp + g) * blk for g in range(grp)]
        qs = [q_ref[pl.ds(pl.multiple_of(b0, blk), blk), :] * scale for b0 in b0s]
        flags = [tile0 * (1 - jnp.minimum(it * grp + g, 1)) for g in range(grp)]
        for b0, (o2, l2) in zip(b0s, attend(qs, kd1, vd1, b0s, flags)):
            op_ref[0, pl.ds(pl.multiple_of(b0, blk), blk), :] = o2
            lp_ref[0, pl.ds(pl.multiple_of(b0, blk), blk), :] = l2
        return carry

    per_it = grp // mid

    def pattern4(it, carry):
        res = [it * per_it + g // mid for g in range(grp)]
        bks = [g % mid for g in range(grp)]
        qs = [q4_ref[pl.ds(pl.multiple_of(r * len4 + b * blk, blk), blk), :] for r, b in zip(res, bks)]
        bases = [r * pitch4 + b * blk for r, b in zip(res, bks)]
        flags = [tile0 if b == 0 else 0 * tile0 for b in bks]
        for r, b, (o2, l2) in zip(res, bks, attend(qs, kd4, vd4, bases, flags)):
            op_ref[1, pl.ds(r + mid * blk * b, blk, stride=mid), :] = o2
            lp_ref[1, pl.ds(r + mid * blk * b, blk, stride=mid), :] = l2
        return carry

    def pattern16(it, carry):
        r4s = [g % mid for g in range(grp)]
        his = [it * per_it + g // mid for g in range(grp)]
        qs = [q4_ref[pl.ds(r4 * len4 + hi, blk, stride=mid), :] for r4, hi in zip(r4s, his)]
        bases = [(r4 + mid * hi) * pitch16 for r4, hi in zip(r4s, his)]
        flags = [tile0] * grp
        for r4, hi, (o2, l2) in zip(r4s, his, attend(qs, kd16, vd16, bases, flags)):
            op_ref[2, pl.ds(r4 + mid * hi, blk, stride=mid * mid), :] = o2
            lp_ref[2, pl.ds(r4 + mid * hi, blk, stride=mid * mid), :] = l2
        return carry

    n_it = tile // blk // grp
    assert grp % mid == 0 and n_it * grp == tile // blk == mid * mid
    lax.fori_loop(0, n_it, pattern1, 0)
    lax.fori_loop(0, n_it, pattern4, 0)
    lax.fori_loop(0, n_it, pattern16, 0)

    for kd, vd, pitch, nres, ln in ((kd1, vd1, 0, 1, tile), (kd4, vd4, pitch4, mid, len4),
                                    (kd16, vd16, pitch16, mid * mid, blk)):
        for r in range(nres):
            kd[r * pitch:r * pitch + blk, :] = kd[r * pitch + ln:r * pitch + ln + blk, :]
            vd[r * pitch:r * pitch + blk, :] = vd[r * pitch + ln:r * pitch + ln + blk, :]

    def merge(c, carry):
        rows = pl.ds(pl.multiple_of(c * CHUNK, CHUNK), CHUNK)
        la, lb, lc = lp_ref[0, rows, :], lp_ref[1, rows, :], lp_ref[2, rows, :]
        m = jnp.maximum(jnp.maximum(la, lb), lc)
        wa, wb, wc = jnp.exp(la - m), jnp.exp(lb - m), jnp.exp(lc - m)
        y_ref[rows, :] = (wa * op_ref[0, rows, :] + wb * op_ref[1, rows, :] + wc * op_ref[2, rows, :]) / (wa + wb + wc)
        return carry

    lax.fori_loop(0, tile // CHUNK, merge, 0)


def _attn(proj4):
    _, b, s, _ = proj4.shape
    tile, blk, mid = ATT_TILE, ATT_SPAN, ATT_MID
    spec = lambda off: pl.BlockSpec((None, None, tile, LANES),
                                    lambda bi, pr, ti: ((BLK_B + off) * SLABS + pr, bi, ti, 0))
    return pl.pallas_call(
        _attn_kernel,
        grid=(b, SLABS, s // tile),
        in_specs=[spec(0), spec(1), spec(2)],
        out_specs=pl.BlockSpec((None, tile, LANES), lambda bi, pr, ti: (bi, ti, pr)),
        out_shape=jax.ShapeDtypeStruct((b, s, GROUP_W), F32),
        scratch_shapes=[pltpu.VMEM((tile, LANES), F32)] * 3
        + [pltpu.VMEM((blk + tile, LANES), BF16)] * 2
        + [pltpu.VMEM((mid * (blk + tile // mid), LANES), BF16)] * 2
        + [pltpu.VMEM((mid * mid * 2 * blk, LANES), BF16)] * 2
        + [pltpu.VMEM((len(ATT_DILATIONS), tile, LANES), F32)] * 2,
        compiler_params=_cparams("parallel", "parallel", "arbitrary"),
        name="dilattn",
    )(proj4, proj4, proj4)


def _causal_conv(x_ref, rows, xpad_ref, w_ref, b_ref):
    c = CHUNK
    for p in range(x_ref.shape[0]):
        xpad_ref[SUBLANES:SUBLANES + c, p * LANES:(p + 1) * LANES] = x_ref[p, rows, :]
    acc = b_ref[...] + w_ref[0:1, :] * xpad_ref[pl.ds(SUBLANES - CONV_W + 1, c), :]
    for j in range(1, CONV_W):
        acc = acc + w_ref[j:j + 1, :] * xpad_ref[pl.ds(SUBLANES - CONV_W + 1 + j, c), :]
    xpad_ref[0:SUBLANES, :] = xpad_ref[c:c + SUBLANES, :]
    return acc


def _pair_lanes(col_fn, first_head):
    return jnp.where(first_head, col_fn(0), col_fn(1))


def _ssd_kernel(z_ref, xbc_ref, dt_ref, cw_ref, cb_ref, dtb_ref, alog_ref, dsk_ref, nw_ref, tril_ref,
                y_ref, xpad_ref, st_ref):
    @pl.when(pl.program_id(1) == 0)
    def _():
        st_ref[...] = jnp.zeros_like(st_ref)
        xpad_ref[0:SUBLANES, :] = jnp.zeros((SUBLANES, xpad_ref.shape[1]), F32)

    lax.fori_loop(0, STEP_CHUNKS, functools.partial(
        _ssd_chunk, z_ref, xbc_ref, dt_ref, cw_ref, cb_ref, dtb_ref, alog_ref, dsk_ref, nw_ref, tril_ref,
        y_ref, xpad_ref, st_ref), 0)


def _ssd_chunk(z_ref, xbc_ref, dt_ref, cw_ref, cb_ref, dtb_ref, alog_ref, dsk_ref, nw_ref, tril_ref,
               y_ref, xpad_ref, st_ref, ci, carry):
    c = CHUNK
    rows = pl.ds(pl.multiple_of(ci * c, c), c)
    xbc = _silu(_causal_conv(xbc_ref, rows, xpad_ref, cw_ref, cb_ref))
    dt = _softplus(dt_ref[rows, :] + dtb_ref[...])
    adt = dt * (-jnp.exp(alog_ref[...]))
    tril01 = tril_ref[...]
    acs = _sum01_matmul(tril01, adt)
    acs_t = acs.T
    e_cs = jnp.exp(acs)
    dt_rest = dt * jnp.exp(acs[c - 1:c, :] - acs)
    causal = tril01 > 0

    lane = lax.broadcasted_iota(jnp.int32, (1, LANES), 1)
    first_head = lane < SSM_HEAD_DIM
    gw = GROUP_W // SSM_GROUPS
    hpg = SSM_HEADS // SSM_GROUPS
    for g in range(SSM_GROUPS):
        bg = xbc[:, GROUP_W + g * SSM_STATE:GROUP_W + (g + 1) * SSM_STATE]
        cg = xbc[:, GROUP_W + (SSM_GROUPS + g) * SSM_STATE:GROUP_W + (SSM_GROUPS + g + 1) * SSM_STATE].astype(BF16)
        cb = jnp.where(causal, _dot_nt(cg, bg.astype(BF16)), 0.0)
        st = st_ref[g]
        y_off = _dot(cg, st.astype(BF16))
        ys, xds, dec_last = [], [], []
        for pr in range(hpg // 2):
            h0 = g * hpg + 2 * pr
            ls = slice(h0 * SSM_HEAD_DIM, (h0 + 2) * SSM_HEAD_DIM)
            xs = xbc[:, ls]
            xdt = (xs * _pair_lanes(lambda i: dt[:, h0 + i:h0 + i + 1], first_head)).astype(BF16)
            yd = []
            for i in range(2):
                hh = h0 + i
                seg = jnp.minimum(acs[:, hh:hh + 1] - acs_t[hh:hh + 1, :], 0.0)
                yd.append(_dot((cb * jnp.exp(seg)).astype(BF16), xdt))
            y_pair = (jnp.where(first_head, yd[0], yd[1])
                      + y_off[:, pr * LANES:(pr + 1) * LANES] * _pair_lanes(lambda i: e_cs[:, h0 + i:h0 + i + 1], first_head)
                      + dsk_ref[:, ls] * xs)
            ys.append(y_pair)
            xds.append((xs * _pair_lanes(lambda i: dt_rest[:, h0 + i:h0 + i + 1], first_head)).astype(BF16))
            dec_last.append(_pair_lanes(lambda i: e_cs[c - 1:c, h0 + i:h0 + i + 1], first_head))
        st_ref[g] = (jnp.concatenate(dec_last, axis=1) * st
                     + _dot_tn(bg.astype(BF16), jnp.concatenate(xds, axis=1)))
        zg = jnp.concatenate([z_ref[g * (gw // LANES) + p, rows, :] for p in range(gw // LANES)], axis=1)
        yg = jnp.concatenate(ys, axis=1) * _silu(zg)
        y_ref[rows, g * gw:(g + 1) * gw] = _rms(yg, nw_ref[:, g * gw:(g + 1) * gw]).astype(y_ref.dtype)
    return carry


def _ssd(proj4, conv_w, conv_b, dt_bias, a_log, d_skip, norm_w):
    _, b, s, _ = proj4.shape
    full = lambda shape: pl.BlockSpec(shape, lambda bi, ci: (0,) * len(shape))
    tril = jnp.asarray(np.tril(np.ones((CHUNK, CHUNK), np.float32)), BF16)
    return pl.pallas_call(
        _ssd_kernel,
        grid=(b, s // STEP_ROWS),
        in_specs=[
            _slab_spec(BLK_CZ, STEP_ROWS),
            _slab_spec(BLK_CXBC, STEP_ROWS, n=SSM_CONV_DIM // LANES),
            pl.BlockSpec((None, None, STEP_ROWS, LANES), lambda bi, ci: (BLK_CDT * SLABS, bi, ci, 0)),
            full((CONV_W, SSM_CONV_DIM)), full((1, SSM_CONV_DIM)), full((1, LANES)), full((1, LANES)),
            full((1, GROUP_W)), full((1, GROUP_W)), full((CHUNK, CHUNK)),
        ],
        out_specs=pl.BlockSpec((None, STEP_ROWS, GROUP_W), lambda bi, ci: (bi, ci, 0)),
        out_shape=jax.ShapeDtypeStruct((b, s, GROUP_W), BF16),
        scratch_shapes=[pltpu.VMEM((CHUNK + SUBLANES, SSM_CONV_DIM), F32),
                        pltpu.VMEM((SSM_GROUPS, SSM_STATE, GROUP_W // SSM_GROUPS), F32)],
        compiler_params=_cparams("parallel", "arbitrary"),
        name="ssd",
    )(proj4, proj4, proj4, conv_w, conv_b, dt_bias, a_log, d_skip, norm_w, tril)


def _gelu_tanh(x):
    return 0.5 * x * (1.0 + jnp.tanh(np.sqrt(2.0 / np.pi).astype(np.float32) * (x + 0.044715 * (x * x * x))))


def _lru_kernel(x_ref, g_ref, cw_ref, cb_ref, wa_ref, ba_ref, wx_ref, bx_ref, ap_ref, nw_ref,
                y_ref, xpad_ref, h_ref):
    @pl.when(pl.program_id(1) == 0)
    def _():
        h_ref[...] = jnp.zeros_like(h_ref)
        xpad_ref[0:SUBLANES, :] = jnp.zeros((SUBLANES, xpad_ref.shape[1]), F32)

    lax.fori_loop(0, STEP_CHUNKS, functools.partial(
        _lru_chunk, x_ref, g_ref, cw_ref, cb_ref, wa_ref, ba_ref, wx_ref, bx_ref, ap_ref, nw_ref,
        y_ref, xpad_ref, h_ref), 0)


def _lru_chunk(x_ref, g_ref, cw_ref, cb_ref, wa_ref, ba_ref, wx_ref, bx_ref, ap_ref, nw_ref,
               y_ref, xpad_ref, h_ref, ci, carry):
    c = CHUNK
    rows = pl.ds(pl.multiple_of(ci * c, c), c)
    xc = _causal_conv(x_ref, rows, xpad_ref, cw_ref, cb_ref)
    xb = xc.astype(BF16)
    r = jax.nn.sigmoid(_dot(xb, wa_ref[...]) + ba_ref[...])
    i = jax.nn.sigmoid(_dot(xb, wx_ref[...]) + bx_ref[...])
    log_a = -LRU_C * r * _softplus(-ap_ref[...])
    a = jnp.exp(log_a)
    bt = jnp.sqrt(jnp.maximum(-_expm1(2.0 * log_a), 0.0)) * (i * xc)
    in_group = lax.broadcasted_iota(jnp.int32, (c, 1), 0) & (SUBLANES - 1)
    d = 1
    while d < SUBLANES:
        keep = in_group >= d
        a_s = pltpu.roll(a, d, 0)
        b_s = pltpu.roll(bt, d, 0)
        bt = jnp.where(keep, a * b_s + bt, bt)
        a = jnp.where(keep, a * a_s, a)
        d *= 2
    h_last = h_ref[...]
    groups = []
    for r in range(c // SUBLANES):
        rows8 = slice(r * SUBLANES, (r + 1) * SUBLANES)
        hr = bt[rows8] + a[rows8] * h_last
        groups.append(hr)
        h_last = hr[SUBLANES - 1:SUBLANES, :]
    h = jnp.concatenate(groups, axis=0)
    h_ref[...] = h_last
    gate = jnp.concatenate([g_ref[p, rows, :] for p in range(SLABS)], axis=1)
    y_ref[rows, :] = _rms(h * _gelu_tanh(gate), nw_ref[...]).astype(y_ref.dtype)
    return carry


def _lru(proj4, conv_w, conv_b, wa_bd, b_a, wx_bd, b_x, a_param, norm_w):
    _, b, s, _ = proj4.shape
    full = lambda shape: pl.BlockSpec(shape, lambda bi, ci: (0,) * len(shape))
    vec = full((1, GROUP_W))
    return pl.pallas_call(
        _lru_kernel,
        grid=(b, s // STEP_ROWS),
        in_specs=[
            _slab_spec(BLK_DX, STEP_ROWS),
            _slab_spec(BLK_DG, STEP_ROWS),
            full((CONV_W, GROUP_W)), vec, full((GROUP_W, GROUP_W)), vec, full((GROUP_W, GROUP_W)), vec, vec, vec,
        ],
        out_specs=pl.BlockSpec((None, STEP_ROWS, GROUP_W), lambda bi, ci: (bi, ci, 0)),
        out_shape=jax.ShapeDtypeStruct((b, s, GROUP_W), BF16),
        scratch_shapes=[pltpu.VMEM((CHUNK + SUBLANES, GROUP_W), F32), pltpu.VMEM((1, GROUP_W), F32)],
        compiler_params=_cparams("parallel", "arbitrary"),
        name="rglru",
    )(proj4, proj4, conv_w, conv_b, wa_bd, b_a, wx_bd, b_x, a_param, norm_w)


def _block_diag(w):
    nb, n, _ = w.shape
    eye = jnp.eye(nb, dtype=w.dtype)
    return jnp.einsum("hij,hg->higj", w, eye).reshape(nb * n, nb * n)


def _pad_lanes(v, width=LANES):
    return jnp.pad(v, (0, width - v.shape[0])).reshape(1, width)


def kernel(x, ffn1_norm, ffn1_w_gate, ffn1_w_up, ffn1_w_down, mix_norm, w_in, w_out, hgrn_lb_logits, hgrn_norm, attn_norm, ssm_conv_w, ssm_conv_b, ssm_dt_bias, ssm_a_log, ssm_d, ssm_norm, lru_conv_w, lru_conv_b, lru_w_a, lru_b_a, lru_w_x, lru_b_x, lru_a_param, lru_norm, ffn2_norm, ffn2_w_gate, ffn2_w_up, ffn2_w_down, final_norm):
    bsz, seq, d = x.shape
    depth = w_in.shape[0]
    t = bsz * seq
    row = lambda v: v.reshape(1, -1).astype(F32)
    x2 = x.reshape(t, d)
    lb_logits = hgrn_lb_logits.astype(F32)
    ffn1_w = (_tile_cols(ffn1_w_gate, FFN_TF, ffn1_norm), _tile_cols(ffn1_w_up, FFN_TF, ffn1_norm),
              _tile_cols(ffn1_w_down, FFN_TN))
    ffn2_w = (_tile_cols(ffn2_w_gate, FFN_TF, ffn2_norm), _tile_cols(ffn2_w_up, FFN_TF, ffn2_norm),
              _tile_cols(ffn2_w_down, FFN_TN))
    w_in_t = _tile_cols(jnp.concatenate(
        [w_in[:, :, :N_IN_BEFORE_PAD],
         jnp.zeros((depth, d, (BLK_DX * GROUP_W) - N_IN_BEFORE_PAD), w_in.dtype),
         w_in[:, :, N_IN_BEFORE_PAD:],
         jnp.zeros((depth, d, INPROJ_W - PROJ_W), w_in.dtype)], axis=2), INPROJ_TN, mix_norm)
    stream = (x2, *_stream_prep(x2))
    for l in range(depth):
        stream = _ffn(stream, *ffn1_w, l)
        x2 = stream[0]
        proj4 = _inproj(stream, w_in_t, l).reshape(INPROJ_W // LANES, bsz, seq, LANES)
        y_a = _hgrn(proj4, lb_logits, row(hgrn_norm[l]), layer=l).reshape(t, GROUP_W)
        y_b = _attn(proj4).reshape(t, GROUP_W)
        y_c = _ssd(proj4, ssm_conv_w[l].astype(F32), row(ssm_conv_b[l]), _pad_lanes(ssm_dt_bias[l].astype(F32)),
                   _pad_lanes(ssm_a_log[l].astype(F32)), row(jnp.repeat(ssm_d[l], SSM_HEAD_DIM)),
                   row(ssm_norm[l])).reshape(t, GROUP_W)
        y_d = _lru(proj4, lru_conv_w[l].astype(F32), row(lru_conv_b[l]), _block_diag(lru_w_a[l]).astype(BF16),
                   row(lru_b_a[l]), _block_diag(lru_w_x[l]).astype(BF16), row(lru_b_x[l]), row(lru_a_param[l]),
                   row(lru_norm[l])).reshape(t, GROUP_W)
        stream = _outproj(x2, y_a, y_b, y_c, y_d, row(attn_norm[l]), w_out[l].astype(BF16))
        stream = _ffn(stream, *ffn2_w, l)
    return _final_norm(stream[0], row(final_norm)).reshape(bsz, seq, d)
```

```python
import functools

import numpy as np
import jax
import jax.numpy as jnp
from jax import lax
from jax.experimental import pallas as pl
from jax.experimental.pallas import tpu as pltpu

F32 = jnp.float32
BF16 = jnp.bfloat16

NORM_EPS = 1e-6
MASK_VALUE = -1e30
GROUP_W = 512
HG_HEADS = 4
HG_HEAD_W = GROUP_W // HG_HEADS
ATT_HEAD_DIM = 64
ATT_SPAN = 128
ATT_DILATIONS = (1, 4, 16)
SSM_HEADS = 8
SSM_HEAD_DIM = 64
SSM_GROUPS = 2
SSM_STATE = 128
SSM_CONV_DIM = GROUP_W + 2 * SSM_GROUPS * SSM_STATE
CONV_W = 4
LRU_C = 8.0

LANES = 128
SUBLANES = 8
VMEM_LIMIT = 60 * 1024 * 1024

PROJ_BLOCKS = 13
PROJ_W = PROJ_BLOCKS * GROUP_W
BLK_A = 0
BLK_B = 4
BLK_CZ = 7
BLK_CXBC = 8
BLK_CDT = 10
BLK_DX = 11
BLK_DG = 12
N_IN_BEFORE_PAD = 10 * GROUP_W + SSM_HEADS
SLABS = GROUP_W // LANES

CHUNK = 256
CHUNK_LEVELS = 8
STEP_CHUNKS = 4
STEP_ROWS = STEP_CHUNKS * CHUNK


def _cparams(*sem):
    return pltpu.CompilerParams(dimension_semantics=sem, vmem_limit_bytes=VMEM_LIMIT)


def _rms(x, w):
    return x * lax.rsqrt(jnp.mean(x * x, axis=-1, keepdims=True) + NORM_EPS) * w


def _silu(x):
    return x * jax.nn.sigmoid(x)


def _softplus(x):
    return jnp.maximum(x, 0.0) + jnp.log1p(jnp.exp(-jnp.abs(x)))


def _expm1(u):
    w = jnp.exp(u)
    near = (jnp.abs(u) < 0.5) & (w != 1.0)
    kahan = (w - 1.0) * u / jnp.log(jnp.where(near, w, 2.0))
    return jnp.where(near, kahan, jnp.where(w == 1.0, u, w - 1.0))


def _dot(a, b):
    return jnp.dot(a, b, preferred_element_type=F32)


def _dot_nt(a, b):
    return lax.dot_general(a, b, (((1,), (1,)), ((), ())), preferred_element_type=F32)


def _dot_tn(a, b):
    return lax.dot_general(a, b, (((0,), (0,)), ((), ())), preferred_element_type=F32)


def _sum01_matmul(m01, x):
    hi = x.astype(BF16)
    r1 = x - hi.astype(F32)
    mid = r1.astype(BF16)
    lo = (r1 - mid.astype(F32)).astype(BF16)
    return _dot(m01, hi) + _dot(m01, mid) + _dot(m01, lo)


def _sum01_matmul2(m01, x):
    hi = x.astype(BF16)
    mid = (x - hi.astype(F32)).astype(BF16)
    both = _dot(m01, jnp.concatenate([hi, mid], axis=1))
    n = x.shape[1]
    return both[:, :n] + both[:, n:]


def _row_scale(r, width):
    return jnp.concatenate([r] * (width // LANES), axis=1)


def _rinv(ssq, d):
    return lax.rsqrt(ssq * (1.0 / d) + NORM_EPS)


def _stream_prep_kernel(x_ref, xb_ref, r_ref):
    x = x_ref[...]
    xb_ref[...] = x.astype(BF16)
    r_ref[...] = jnp.broadcast_to(_rinv(jnp.sum(x * x, axis=-1, keepdims=True), x.shape[1]), r_ref.shape)


def _stream_prep(x2, *, tm=1024):
    t, d = x2.shape
    return pl.pallas_call(
        _stream_prep_kernel,
        grid=(t // tm,),
        in_specs=[pl.BlockSpec((tm, d), lambda i: (i, 0))],
        out_specs=[pl.BlockSpec((tm, d), lambda i: (i, 0)), pl.BlockSpec((tm, LANES), lambda i: (i, 0))],
        out_shape=[jax.ShapeDtypeStruct((t, d), BF16), jax.ShapeDtypeStruct((t, LANES), F32)],
        compiler_params=_cparams("parallel"),
        name="stream_prep",
    )(x2)


def _ffn_kernel(xb_ref, r_ref, xc_ref, wg_ref, wu_ref, wd_ref, o_ref, ob_ref, ro_ref, a_ref, ssq_ref,
                *, n_f, tf, d):
    j = pl.program_id(1)

    @pl.when(j < n_f)
    def _():
        xb = xb_ref[...]
        rt = _row_scale(r_ref[...], tf)
        a_ref[j] = (_silu(_dot(xb, wg_ref[...]) * rt) * (_dot(xb, wu_ref[...]) * rt)).astype(BF16)

    @pl.when(j >= n_f)
    def _():
        acc = _dot(a_ref[0], wd_ref[0:tf, :])
        for jj in range(1, n_f):
            acc = acc + _dot(a_ref[jj], wd_ref[jj * tf:(jj + 1) * tf, :])
        y = xc_ref[...] + 0.5 * acc
        o_ref[...] = y
        ob_ref[...] = y.astype(BF16)
        part = jnp.broadcast_to(jnp.sum(y * y, axis=-1, keepdims=True), ssq_ref.shape)
        ssq = part + jnp.where(j == n_f, 0.0, ssq_ref[...])
        ssq_ref[...] = ssq
        ro_ref[...] = _rinv(ssq, d)


FFN_TM = 1024
FFN_TF = 512
FFN_TN = 512


def _tile_cols_kernel(w_ref, g_ref, o_ref):
    o_ref[...] = (w_ref[...] * g_ref[...]).astype(BF16)


def _tile_cols(w, tile, gain=None):
    nl, k, n = w.shape
    kb = k // 2
    g = jnp.ones((nl, k), F32) if gain is None else gain.astype(F32)
    return pl.pallas_call(
        _tile_cols_kernel,
        grid=(nl, n // tile, k // kb),
        in_specs=[pl.BlockSpec((None, kb, tile), lambda l, j, kk: (l, kk, j)),
                  pl.BlockSpec((None, kb, 1), lambda l, j, kk: (l, kk, 0))],
        out_specs=pl.BlockSpec((None, None, kb, tile), lambda l, j, kk: (l, j, kk, 0)),
        out_shape=jax.ShapeDtypeStruct((nl, n // tile, k, tile), BF16),
        compiler_params=_cparams("parallel", "parallel", "parallel"),
        name="tile_cols",
    )(w.astype(F32), g.reshape(nl, k, 1))


def _ffn(stream, wg_t, wu_t, wd_t, layer, *, tm=FFN_TM):
    x2, xb, r = stream
    t, d = x2.shape
    _, n_f, _, tf = wg_t.shape
    _, n_d, f, tn = wd_t.shape
    up = lambda i, j: (layer, jnp.minimum(j, n_f - 1), 0, 0)
    down = lambda i, j: (layer, jnp.maximum(j - n_f, 0), 0, 0)
    col = lambda i, j: (i, jnp.maximum(j - n_f, 0))
    rows = lambda i, j: (i, 0)
    return pl.pallas_call(
        functools.partial(_ffn_kernel, n_f=n_f, tf=tf, d=d),
        grid=(t // tm, n_f + n_d),
        in_specs=[
            pl.BlockSpec((tm, d), rows),
            pl.BlockSpec((tm, LANES), rows),
            pl.BlockSpec((tm, tn), col),
            pl.BlockSpec((None, None, d, tf), up),
            pl.BlockSpec((None, None, d, tf), up),
            pl.BlockSpec((None, None, f, tn), down),
        ],
        out_specs=[pl.BlockSpec((tm, tn), col), pl.BlockSpec((tm, tn), col), pl.BlockSpec((tm, LANES), rows)],
        out_shape=[jax.ShapeDtypeStruct((t, d), F32), jax.ShapeDtypeStruct((t, d), BF16),
                   jax.ShapeDtypeStruct((t, LANES), F32)],
        scratch_shapes=[pltpu.VMEM((n_f, tm, tf), BF16), pltpu.VMEM((tm, LANES), F32)],
        compiler_params=_cparams("parallel", "arbitrary"),
        name="ffn",
    )(xb, r, x2, wg_t, wu_t, wd_t)


def _final_norm_kernel(x_ref, w_ref, o_ref):
    o_ref[...] = _rms(x_ref[...], w_ref[...])


def _final_norm(x2, w, *, tm=1024):
    t, d = x2.shape
    return pl.pallas_call(
        _final_norm_kernel,
        grid=(t // tm,),
        in_specs=[pl.BlockSpec((tm, d), lambda i: (i, 0)), pl.BlockSpec((1, d), lambda i: (0, 0))],
        out_specs=pl.BlockSpec((tm, d), lambda i: (i, 0)),
        out_shape=jax.ShapeDtypeStruct((t, d), F32),
        compiler_params=_cparams("parallel"),
        name="final_norm",
    )(x2, w)


def _inproj_kernel(xb_ref, r_ref, w_ref, o_ref):
    r = r_ref[...]
    res = _dot(xb_ref[...], w_ref[...])
    for p in range(o_ref.shape[0]):
        o_ref[p] = res[:, p * LANES:(p + 1) * LANES] * r


INPROJ_TN = 2304
INPROJ_W = 3 * INPROJ_TN
assert INPROJ_W >= PROJ_W and INPROJ_TN % LANES == 0


def _inproj(stream, w_t, layer, *, tm=1024):
    _, xb, r = stream
    t, d = xb.shape
    _, n_t, _, tn = w_t.shape
    return pl.pallas_call(
        _inproj_kernel,
        grid=(t // tm, n_t),
        in_specs=[
            pl.BlockSpec((tm, d), lambda i, j: (i, 0)),
            pl.BlockSpec((tm, LANES), lambda i, j: (i, 0)),
            pl.BlockSpec((None, None, d, tn), lambda i, j: (layer, j, 0, 0)),
        ],
        out_specs=pl.BlockSpec((tn // LANES, tm, LANES), lambda i, j: (j, i, 0)),
        out_shape=jax.ShapeDtypeStruct((n_t * tn // LANES, t, LANES), F32),
        compiler_params=_cparams("parallel", "arbitrary"),
        name="inproj",
    )(xb, r, w_t)


def _outproj_kernel(x_ref, ya_ref, yb_ref, yc_ref, yd_ref, bnw_ref, w_ref, o_ref, ob_ref, ro_ref):
    yb = _rms(yb_ref[...], bnw_ref[...]).astype(BF16)
    acc = x_ref[...]
    for g, y in enumerate((ya_ref[...], yb, yc_ref[...], yd_ref[...])):
        acc = acc + _dot(y, w_ref[g * GROUP_W:(g + 1) * GROUP_W, :])
    o_ref[...] = acc
    ob_ref[...] = acc.astype(BF16)
    ro_ref[...] = jnp.broadcast_to(_rinv(jnp.sum(acc * acc, axis=-1, keepdims=True), acc.shape[1]), ro_ref.shape)


def _outproj(x2, ya, yb, yc, yd, attn_norm_w, w, *, tm=512):
    t, d = x2.shape
    yspec = pl.BlockSpec((tm, GROUP_W), lambda i: (i, 0))
    rows = pl.BlockSpec((tm, d), lambda i: (i, 0))
    return pl.pallas_call(
        _outproj_kernel,
        grid=(t // tm,),
        in_specs=[rows, yspec, yspec, yspec, yspec,
                  pl.BlockSpec((1, GROUP_W), lambda i: (0, 0)), pl.BlockSpec((4 * GROUP_W, d), lambda i: (0, 0))],
        out_specs=[rows, rows, pl.BlockSpec((tm, LANES), lambda i: (i, 0))],
        out_shape=[jax.ShapeDtypeStruct((t, d), F32), jax.ShapeDtypeStruct((t, d), BF16),
                   jax.ShapeDtypeStruct((t, LANES), F32)],
        compiler_params=_cparams("parallel"),
        name="outproj",
    )(x2, ya, yb, yc, yd, attn_norm_w, w)


def _hgrn_tables(c=CHUNK, levels=CHUNK_LEVELS):
    t = np.arange(c)
    rows = []
    for lvl in range(levels):
        h = 1 << lvl
        r = (t // (2 * h)) * (2 * h) + h
        up = (t // h) % 2 == 1
        u = t[None, :]
        m = np.where(up[:, None], (u > r[:, None]) & (u <= t[:, None]), (u > t[:, None]) & (u <= r[:, None]))
        rows.append(m)
    rows.append(t[None, :] <= t[:, None])
    rows.append(t[None, :] > t[:, None])
    mstack = np.concatenate(rows, axis=0).astype(np.float32)
    x = t[:, None] ^ t[None, :]
    lv = np.where(t[None, :] < t[:, None], np.floor(np.log2(np.maximum(x, 1))).astype(np.int32), -1)
    return mstack, lv.astype(np.int32)


_HGRN_MSTACK, _HGRN_LEVEL = _hgrn_tables()


def _hgrn_kernel(q_ref, f_ref, i_ref, g_ref, lbl_ref, nw_ref, mst_ref, lv_ref, o_ref, st_ref, *, layer):
    c = CHUNK

    @pl.when(pl.program_id(1) == 0)
    def _():
        st_ref[...] = jnp.zeros_like(st_ref)

    lbl = lbl_ref[...]
    e = jnp.exp(lbl - jnp.max(lbl, axis=0, keepdims=True))
    p = e / jnp.sum(e, axis=0, keepdims=True)
    lb = jnp.zeros((1, GROUP_W), F32)
    for i in range(1, layer + 1):
        lb = lb + p[i:i + 1, :]

    lax.fori_loop(0, STEP_CHUNKS, functools.partial(
        _hgrn_chunk, q_ref, f_ref, i_ref, g_ref, nw_ref, mst_ref, lv_ref, o_ref, st_ref, lb), 0)


def _hgrn_chunk(q_ref, f_ref, i_ref, g_ref, nw_ref, mst_ref, lv_ref, o_ref, st_ref, lb, ci, carry):
    c = CHUNK
    rows = pl.ds(pl.multiple_of(ci * c, c), c)
    mst = mst_ref[...]
    lv = lv_ref[...]
    row = lax.broadcasted_iota(jnp.int32, (c, 1), 0)
    for h in range(HG_HEADS):
        sl = slice(h * HG_HEAD_W, (h + 1) * HG_HEAD_W)
        q = _silu(q_ref[h, rows, :])
        z = f_ref[h, rows, :]
        v = i_ref[h, rows, :]
        lbh = lb[:, sl]
        g = jnp.log(lbh + (1.0 - lbh) * jax.nn.sigmoid(z))
        k = (1.0 - lbh) * jax.nn.sigmoid(-z)
        ex = jnp.exp(_sum01_matmul2(mst, g))
        a = jnp.zeros((c, c), F32)
        for lvl in range(CHUNK_LEVELS):
            exl = ex[lvl * c:(lvl + 1) * c]
            up = ((row >> lvl) & 1) == 1
            qp = jnp.where(up, q * exl, 0.0).astype(BF16)
            kp = jnp.where(up, 0.0, k * exl).astype(BF16)
            a = jnp.where(lv == lvl, _dot_nt(qp, kp), a)
        vb = v.astype(BF16)
        ex_b = ex[CHUNK_LEVELS * c:(CHUNK_LEVELS + 1) * c]
        ex_r = ex[(CHUNK_LEVELS + 1) * c:(CHUNK_LEVELS + 2) * c]
        st = st_ref[h]
        o = (_dot(a.astype(BF16), vb) + jnp.sum(q * k, axis=-1, keepdims=True) * v
             + _dot_nt((q * ex_b).astype(BF16), st.astype(BF16)))
        st_ref[h] = ex_b[c - 1:c, :] * st + _dot_tn(vb, (k * ex_r).astype(BF16))
        o = _rms(o, nw_ref[:, sl]) * _silu(g_ref[h, rows, :])
        o_ref[rows, sl] = o.astype(o_ref.dtype)
    return carry


def _slab_spec(blk, rows, n=SLABS):
    first = blk * SLABS // n
    return pl.BlockSpec((n, None, rows, LANES), lambda bi, ci: (first, bi, ci, 0))


def _hgrn(proj4, lb_logits, norm_w, *, layer):
    _, b, s, _ = proj4.shape
    cspec = lambda blk: _slab_spec(blk, STEP_ROWS)
    full = lambda shape: pl.BlockSpec(shape, lambda bi, ci: (0,) * len(shape))
    mst = jnp.asarray(_HGRN_MSTACK, BF16)
    lv = jnp.asarray(_HGRN_LEVEL)
    return pl.pallas_call(
        functools.partial(_hgrn_kernel, layer=layer),
        grid=(b, s // STEP_ROWS),
        in_specs=[cspec(BLK_A), cspec(BLK_A + 1), cspec(BLK_A + 2), cspec(BLK_A + 3),
                  full(lb_logits.shape), full((1, GROUP_W)), full(mst.shape), full(lv.shape)],
        out_specs=pl.BlockSpec((None, STEP_ROWS, GROUP_W), lambda bi, ci: (bi, ci, 0)),
        out_shape=jax.ShapeDtypeStruct((b, s, GROUP_W), BF16),
        scratch_shapes=[pltpu.VMEM((HG_HEADS, HG_HEAD_W, HG_HEAD_W), F32)],
        compiler_params=_cparams("parallel", "arbitrary"),
        name="hgrn2",
    )(proj4, proj4, proj4, proj4, lb_logits, norm_w, mst, lv)


ATT_TILE = 2048
ATT_GROUP = 16
ATT_MID = 4
assert ATT_DILATIONS == (1, ATT_MID, ATT_MID * ATT_MID) and ATT_TILE == ATT_SPAN * ATT_DILATIONS[-1]


def _attend_group(qs, ks, vs, first_flags, bias_all, bias_first, first_head):
    blk = ATT_SPAN
    scores = []
    for q2, k2, ff in zip(qs, ks, first_flags):
        qq = jnp.concatenate([jnp.where(first_head, q2, 0.0), jnp.where(first_head, 0.0, q2)], axis=0).astype(BF16)
        bias = jnp.where(ff > 0, bias_first, bias_all)
        scores.append(_dot_nt(qq, k2) + jnp.concatenate([bias, bias], axis=0))
    s = jnp.concatenate(scores, axis=0)
    m = jnp.max(s, axis=-1, keepdims=True)
    p = jnp.exp(s - m)
    den = jnp.sum(p, axis=-1, keepdims=True)
    lse = m + jnp.log(den)
    pb = p.astype(BF16)
    outs = []
    for g, v2 in enumerate(vs):
        r0 = g * 2 * blk
        pv = _dot(pb[r0:r0 + 2 * blk], v2)
        d0, d1 = den[r0:r0 + blk], den[r0 + blk:r0 + 2 * blk]
        l0, l1 = lse[r0:r0 + blk], lse[r0 + blk:r0 + 2 * blk]
        o2 = jnp.where(first_head, pv[:blk], pv[blk:]) / jnp.where(first_head, d0, d1)
        outs.append((o2, jnp.where(first_head, l0, l1)))
    return outs


def _attn_kernel(q_ref, k_ref, v_ref, y_ref, q4_ref, k4_ref, v4_ref, kd1, vd1, kd4, vd4, kd16, vd16, op_ref, lp_ref):
    i = pl.program_id(2)
    blk, tile, mid = ATT_SPAN, ATT_TILE, ATT_MID
    len4 = tile // mid
    pitch4 = blk + len4
    pitch16 = 2 * blk
    scale = ATT_HEAD_DIM ** -0.5
    zeros = jnp.zeros((blk, LANES), BF16)

    @pl.when(i == 0)
    def _():
        for kd, vd, pitch, nres in ((kd1, vd1, 0, 1), (kd4, vd4, pitch4, mid), (kd16, vd16, pitch16, mid * mid)):
            for r in range(nres):
                kd[r * pitch:r * pitch + blk, :] = zeros
                vd[r * pitch:r * pitch + blk, :] = zeros

    kd1[blk:blk + tile, :] = k_ref[...].astype(BF16)
    vd1[blk:blk + tile, :] = v_ref[...].astype(BF16)
    for r in range(mid):
        rows = slice(r * len4, (r + 1) * len4)
        cur = slice(r * pitch4 + blk, (r + 1) * pitch4)
        kk = k_ref[pl.ds(r, len4, stride=mid), :]
        vv = v_ref[pl.ds(r, len4, stride=mid), :]
        k4_ref[rows, :] = kk
        v4_ref[rows, :] = vv
        kd4[cur, :] = kk.astype(BF16)
        vd4[cur, :] = vv.astype(BF16)
        q4_ref[rows, :] = q_ref[pl.ds(r, len4, stride=mid), :] * scale
    for r4 in range(mid):
        for hi in range(mid):
            r16 = r4 + mid * hi
            cur = slice(r16 * pitch16 + blk, (r16 + 1) * pitch16)
            kd16[cur, :] = k4_ref[pl.ds(r4 * len4 + hi, blk, stride=mid), :].astype(BF16)
            vd16[cur, :] = v4_ref[pl.ds(r4 * len4 + hi, blk, stride=mid), :].astype(BF16)

    iq = lax.broadcasted_iota(jnp.int32, (blk, 2 * blk), 0)
    ik = lax.broadcasted_iota(jnp.int32, (blk, 2 * blk), 1)
    dist = iq + blk - ik
    valid = (dist >= 0) & (dist <= blk)
    bias_all = jnp.where(valid, 0.0, MASK_VALUE)
    bias_first = jnp.where(valid & (ik >= blk), 0.0, MASK_VALUE)
    lane = lax.broadcasted_iota(jnp.int32, (1, LANES), 1)
    first_head = lane < ATT_HEAD_DIM
    tile0 = 1 - jnp.minimum(i, 1)
    grp = ATT_GROUP

    def attend(qs, kd, vd, bases, flags):
        ks = [kd[pl.ds(pl.multiple_of(b0, blk), 2 * blk), :] for b0 in bases]
        vs = [vd[pl.ds(pl.multiple_of(b0, blk), 2 * blk), :] for b0 in bases]
        return _attend_group(qs, ks, vs, flags, bias_all, bias_first, first_head)

    def pattern1(it, carry):
        b0s = [(it * grp + g) * blk for g in range(grp)]
        qs = [q_ref[pl.ds(pl.multiple_of(b0, blk), blk), :] * scale for b0 in b0s]
        flags = [tile0 * (1 - jnp.minimum(it * grp + g, 1)) for g in range(grp)]
        for b0, (o2, l2) in zip(b0s, attend(qs, kd1, vd1, b0s, flags)):
            op_ref[0, pl.ds(pl.multiple_of(b0, blk), blk), :] = o2
            lp_ref[0, pl.ds(pl.multiple_of(b0, blk), blk), :] = l2
        return carry

    per_it = grp // mid

    def pattern4(it, carry):
        res = [it * per_it + g // mid for g in range(grp)]
        bks = [g % mid for g in range(grp)]
        qs = [q4_ref[pl.ds(pl.multiple_of(r * len4 + b * blk, blk), blk), :] for r, b in zip(res, bks)]
        bases = [r * pitch4 + b * blk for r, b in zip(res, bks)]
        flags = [tile0 if b == 0 else 0 * tile0 for b in bks]
        for r, b, (o2, l2) in zip(res, bks, attend(qs, kd4, vd4, bases, flags)):
            op_ref[1, pl.ds(r + mid * blk * b, blk, stride=mid), :] = o2
            lp_ref[1, pl.ds(r + mid * blk * b, blk, stride=mid), :] = l2
        return carry

    def pattern16(it, carry):
        r4s = [g % mid for g in range(grp)]
        his = [it * per_it + g // mid for g in range(grp)]
        qs = [q4_ref[pl.ds(r4 * len4 + hi, blk, stride=mid), :] for r4, hi in zip(r4s, his)]
        bases = [(r4 + mid * hi) * pitch16 for r4, hi in zip(r4s, his)]
        flags = [tile0] * grp
        for r4, hi, (o2, l2) in zip(r4s, his, attend(qs, kd16, vd16, bases, flags)):
            op_ref[2, pl.ds(r4 + mid * hi, blk, stride=mid * mid), :] = o2
            lp_ref[2, pl.ds(r4 + mid * hi, blk, stride=mid * mid), :] = l2
        return carry

    n_it = tile // blk // grp
    assert grp % mid == 0 and n_it * grp == tile // blk == mid * mid
    lax.fori_loop(0, n_it, pattern1, 0)
    lax.fori_loop(0, n_it, pattern4, 0)
    lax.fori_loop(0, n_it, pattern16, 0)

    for kd, vd, pitch, nres, ln in ((kd1, vd1, 0, 1, tile), (kd4, vd4, pitch4, mid, len4),
                                    (kd16, vd16, pitch16, mid * mid, blk)):
        for r in range(nres):
            kd[r * pitch:r * pitch + blk, :] = kd[r * pitch + ln:r * pitch + ln + blk, :]
            vd[r * pitch:r * pitch + blk, :] = vd[r * pitch + ln:r * pitch + ln + blk, :]

    def merge(c, carry):
        rows = pl.ds(pl.multiple_of(c * CHUNK, CHUNK), CHUNK)
        la, lb, lc = lp_ref[0, rows, :], lp_ref[1, rows, :], lp_ref[2, rows, :]
        m = jnp.maximum(jnp.maximum(la, lb), lc)
        wa, wb, wc = jnp.exp(la - m), jnp.exp(lb - m), jnp.exp(lc - m)
        y_ref[rows, :] = (wa * op_ref[0, rows, :] + wb * op_ref[1, rows, :] + wc * op_ref[2, rows, :]) / (wa + wb + wc)
        return carry

    lax.fori_loop(0, tile // CHUNK, merge, 0)


def _attn(proj4):
    _, b, s, _ = proj4.shape
    tile, blk, mid = ATT_TILE, ATT_SPAN, ATT_MID
    spec = lambda off: pl.BlockSpec((None, None, tile, LANES),
                                    lambda bi, pr, ti: ((BLK_B + off) * SLABS + pr, bi, ti, 0))
    return pl.pallas_call(
        _attn_kernel,
        grid=(b, SLABS, s // tile),
        in_specs=[spec(0), spec(1), spec(2)],
        out_specs=pl.BlockSpec((None, tile, LANES), lambda bi, pr, ti: (bi, ti, pr)),
        out_shape=jax.ShapeDtypeStruct((b, s, GROUP_W), F32),
        scratch_shapes=[pltpu.VMEM((tile, LANES), F32)] * 3
        + [pltpu.VMEM((blk + tile, LANES), BF16)] * 2
        + [pltpu.VMEM((mid * (blk + tile // mid), LANES), BF16)] * 2
        + [pltpu.VMEM((mid * mid * 2 * blk, LANES), BF16)] * 2
        + [pltpu.VMEM((len(ATT_DILATIONS), tile, LANES), F32)] * 2,
        compiler_params=_cparams("parallel", "parallel", "arbitrary"),
        name="dilattn",
    )(proj4, proj4, proj4)


def _causal_conv(x_ref, rows, xpad_ref, w_ref, b_ref):
    c = CHUNK
    for p in range(x_ref.shape[0]):
        xpad_ref[SUBLANES:SUBLANES + c, p * LANES:(p + 1) * LANES] = x_ref[p, rows, :]
    acc = b_ref[...] + w_ref[0:1, :] * xpad_ref[pl.ds(SUBLANES - CONV_W + 1, c), :]
    for j in range(1, CONV_W):
        acc = acc + w_ref[j:j + 1, :] * xpad_ref[pl.ds(SUBLANES - CONV_W + 1 + j, c), :]
    xpad_ref[0:SUBLANES, :] = xpad_ref[c:c + SUBLANES, :]
    return acc


def _pair_lanes(col_fn, first_head):
    return jnp.where(first_head, col_fn(0), col_fn(1))


def _ssd_kernel(z_ref, xbc_ref, dt_ref, cw_ref, cb_ref, dtb_ref, alog_ref, dsk_ref, nw_ref, tril_ref,
                y_ref, xpad_ref, st_ref):
    @pl.when(pl.program_id(1) == 0)
    def _():
        st_ref[...] = jnp.zeros_like(st_ref)
        xpad_ref[0:SUBLANES, :] = jnp.zeros((SUBLANES, xpad_ref.shape[1]), F32)

    lax.fori_loop(0, STEP_CHUNKS, functools.partial(
        _ssd_chunk, z_ref, xbc_ref, dt_ref, cw_ref, cb_ref, dtb_ref, alog_ref, dsk_ref, nw_ref, tril_ref,
        y_ref, xpad_ref, st_ref), 0)


def _ssd_chunk(z_ref, xbc_ref, dt_ref, cw_ref, cb_ref, dtb_ref, alog_ref, dsk_ref, nw_ref, tril_ref,
               y_ref, xpad_ref, st_ref, ci, carry):
    c = CHUNK
    rows = pl.ds(pl.multiple_of(ci * c, c), c)
    xbc = _silu(_causal_conv(xbc_ref, rows, xpad_ref, cw_ref, cb_ref))
    dt = _softplus(dt_ref[rows, :] + dtb_ref[...])
    adt = dt * (-jnp.exp(alog_ref[...]))
    tril01 = tril_ref[...]
    acs = _sum01_matmul(tril01, adt)
    acs_t = acs.T
    e_cs = jnp.exp(acs)
    dt_rest = dt * jnp.exp(acs[c - 1:c, :] - acs)
    causal = tril01 > 0

    lane = lax.broadcasted_iota(jnp.int32, (1, LANES), 1)
    first_head = lane < SSM_HEAD_DIM
    gw = GROUP_W // SSM_GROUPS
    hpg = SSM_HEADS // SSM_GROUPS
    for g in range(SSM_GROUPS):
        bg = xbc[:, GROUP_W + g * SSM_STATE:GROUP_W + (g + 1) * SSM_STATE]
        cg = xbc[:, GROUP_W + (SSM_GROUPS + g) * SSM_STATE:GROUP_W + (SSM_GROUPS + g + 1) * SSM_STATE].astype(BF16)
        cb = jnp.where(causal, _dot_nt(cg, bg.astype(BF16)), 0.0)
        st = st_ref[g]
        y_off = _dot(cg, st.astype(BF16))
        ys, xds, dec_last = [], [], []
        for pr in range(hpg // 2):
            h0 = g * hpg + 2 * pr
            ls = slice(h0 * SSM_HEAD_DIM, (h0 + 2) * SSM_HEAD_DIM)
            xs = xbc[:, ls]
            xdt = (xs * _pair_lanes(lambda i: dt[:, h0 + i:h0 + i + 1], first_head)).astype(BF16)
            yd = []
            for i in range(2):
                hh = h0 + i
                seg = jnp.minimum(acs[:, hh:hh + 1] - acs_t[hh:hh + 1, :], 0.0)
                yd.append(_dot((cb * jnp.exp(seg)).astype(BF16), xdt))
            y_pair = (jnp.where(first_head, yd[0], yd[1])
                      + y_off[:, pr * LANES:(pr + 1) * LANES] * _pair_lanes(lambda i: e_cs[:, h0 + i:h0 + i + 1], first_head)
                      + dsk_ref[:, ls] * xs)
            ys.append(y_pair)
            xds.append((xs * _pair_lanes(lambda i: dt_rest[:, h0 + i:h0 + i + 1], first_head)).astype(BF16))
            dec_last.append(_pair_lanes(lambda i: e_cs[c - 1:c, h0 + i:h0 + i + 1], first_head))
        st_ref[g] = (jnp.concatenate(dec_last, axis=1) * st
                     + _dot_tn(bg.astype(BF16), jnp.concatenate(xds, axis=1)))
        zg = jnp.concatenate([z_ref[g * (gw // LANES) + p, rows, :] for p in range(gw // LANES)], axis=1)
        yg = jnp.concatenate(ys, axis=1) * _silu(zg)
        y_ref[rows, g * gw:(g + 1) * gw] = _rms(yg, nw_ref[:, g * gw:(g + 1) * gw]).astype(y_ref.dtype)
    return carry


def _ssd(proj4, conv_w, conv_b, dt_bias, a_log, d_skip, norm_w):
    _, b, s, _ = proj4.shape
    full = lambda shape: pl.BlockSpec(shape, lambda bi, ci: (0,) * len(shape))
    tril = jnp.asarray(np.tril(np.ones((CHUNK, CHUNK), np.float32)), BF16)
    return pl.pallas_call(
        _ssd_kernel,
        grid=(b, s // STEP_ROWS),
        in_specs=[
            _slab_spec(BLK_CZ, STEP_ROWS),
            _slab_spec(BLK_CXBC, STEP_ROWS, n=SSM_CONV_DIM // LANES),
            pl.BlockSpec((None, None, STEP_ROWS, LANES), lambda bi, ci: (BLK_CDT * SLABS, bi, ci, 0)),
            full((CONV_W, SSM_CONV_DIM)), full((1, SSM_CONV_DIM)), full((1, LANES)), full((1, LANES)),
            full((1, GROUP_W)), full((1, GROUP_W)), full((CHUNK, CHUNK)),
        ],
        out_specs=pl.BlockSpec((None, STEP_ROWS, GROUP_W), lambda bi, ci: (bi, ci, 0)),
        out_shape=jax.ShapeDtypeStruct((b, s, GROUP_W), BF16),
        scratch_shapes=[pltpu.VMEM((CHUNK + SUBLANES, SSM_CONV_DIM), F32),
                        pltpu.VMEM((SSM_GROUPS, SSM_STATE, GROUP_W // SSM_GROUPS), F32)],
        compiler_params=_cparams("parallel", "arbitrary"),
        name="ssd",
    )(proj4, proj4, proj4, conv_w, conv_b, dt_bias, a_log, d_skip, norm_w, tril)


def _gelu_tanh(x):
    return 0.5 * x * (1.0 + jnp.tanh(np.sqrt(2.0 / np.pi).astype(np.float32) * (x + 0.044715 * (x * x * x))))


def _lru_kernel(x_ref, g_ref, cw_ref, cb_ref, wa_ref, ba_ref, wx_ref, bx_ref, ap_ref, nw_ref,
                y_ref, xpad_ref, h_ref):
    @pl.when(pl.program_id(1) == 0)
    def _():
        h_ref[...] = jnp.zeros_like(h_ref)
        xpad_ref[0:SUBLANES, :] = jnp.zeros((SUBLANES, xpad_ref.shape[1]), F32)

    lax.fori_loop(0, STEP_CHUNKS, functools.partial(
        _lru_chunk, x_ref, g_ref, cw_ref, cb_ref, wa_ref, ba_ref, wx_ref, bx_ref, ap_ref, nw_ref,
        y_ref, xpad_ref, h_ref), 0)


def _lru_chunk(x_ref, g_ref, cw_ref, cb_ref, wa_ref, ba_ref, wx_ref, bx_ref, ap_ref, nw_ref,
               y_ref, xpad_ref, h_ref, ci, carry):
    c = CHUNK
    rows = pl.ds(pl.multiple_of(ci * c, c), c)
    xc = _causal_conv(x_ref, rows, xpad_ref, cw_ref, cb_ref)
    xb = xc.astype(BF16)
    r = jax.nn.sigmoid(_dot(xb, wa_ref[...]) + ba_ref[...])
    i = jax.nn.sigmoid(_dot(xb, wx_ref[...]) + bx_ref[...])
    log_a = -LRU_C * r * _softplus(-ap_ref[...])
    a = jnp.exp(log_a)
    bt = jnp.sqrt(jnp.maximum(-_expm1(2.0 * log_a), 0.0)) * (i * xc)
    in_group = lax.broadcasted_iota(jnp.int32, (c, 1), 0) & (SUBLANES - 1)
    d = 1
    while d < SUBLANES:
        keep = in_group >= d
        a_s = pltpu.roll(a, d, 0)
        b_s = pltpu.roll(bt, d, 0)
        bt = jnp.where(keep, a * b_s + bt, bt)
        a = jnp.where(keep, a * a_s, a)
        d *= 2
    h_last = h_ref[...]
    groups = []
    for r in range(c // SUBLANES):
        rows8 = slice(r * SUBLANES, (r + 1) * SUBLANES)
        hr = bt[rows8] + a[rows8] * h_last
        groups.append(hr)
        h_last = hr[SUBLANES - 1:SUBLANES, :]
    h = jnp.concatenate(groups, axis=0)
    h_ref[...] = h_last
    gate = jnp.concatenate([g_ref[p, rows, :] for p in range(SLABS)], axis=1)
    y_ref[rows, :] = _rms(h * _gelu_tanh(gate), nw_ref[...]).astype(y_ref.dtype)
    return carry


def _lru(proj4, conv_w, conv_b, wa_bd, b_a, wx_bd, b_x, a_param, norm_w):
    _, b, s, _ = proj4.shape
    full = lambda shape: pl.BlockSpec(shape, lambda bi, ci: (0,) * len(shape))
    vec = full((1, GROUP_W))
    return pl.pallas_call(
        _lru_kernel,
        grid=(b, s // STEP_ROWS),
        in_specs=[
            _slab_spec(BLK_DX, STEP_ROWS),
            _slab_spec(BLK_DG, STEP_ROWS),
            full((CONV_W, GROUP_W)), vec, full((GROUP_W, GROUP_W)), vec, full((GROUP_W, GROUP_W)), vec, vec, vec,
        ],
        out_specs=pl.BlockSpec((None, STEP_ROWS, GROUP_W), lambda bi, ci: (bi, ci, 0)),
        out_shape=jax.ShapeDtypeStruct((b, s, GROUP_W), BF16),
        scratch_shapes=[pltpu.VMEM((CHUNK + SUBLANES, GROUP_W), F32), pltpu.VMEM((1, GROUP_W), F32)],
        compiler_params=_cparams("parallel", "arbitrary"),
        name="rglru",
    )(proj4, proj4, conv_w, conv_b, wa_bd, b_a, wx_bd, b_x, a_param, norm_w)


def _block_diag(w):
    nb, n, _ = w.shape
    eye = jnp.eye(nb, dtype=w.dtype)
    return jnp.einsum("hij,hg->higj", w, eye).reshape(nb * n, nb * n)


def _pad_lanes(v, width=LANES):
    return jnp.pad(v, (0, width - v.shape[0])).reshape(1, width)


def kernel(x, ffn1_norm, ffn1_w_gate, ffn1_w_up, ffn1_w_down, mix_norm, w_in, w_out, hgrn_lb_logits, hgrn_norm, attn_norm, ssm_conv_w, ssm_conv_b, ssm_dt_bias, ssm_a_log, ssm_d, ssm_norm, lru_conv_w, lru_conv_b, lru_w_a, lru_b_a, lru_w_x, lru_b_x, lru_a_param, lru_norm, ffn2_norm, ffn2_w_gate, ffn2_w_up, ffn2_w_down, final_norm):
    bsz, seq, d = x.shape
    depth = w_in.shape[0]
    t = bsz * seq
    row = lambda v: v.reshape(1, -1).astype(F32)
    x2 = x.reshape(t, d)
    lb_logits = hgrn_lb_logits.astype(F32)
    ffn1_w = (_tile_cols(ffn1_w_gate, FFN_TF, ffn1_norm), _tile_cols(ffn1_w_up, FFN_TF, ffn1_norm),
              _tile_cols(ffn1_w_down, FFN_TN))
    ffn2_w = (_tile_cols(ffn2_w_gate, FFN_TF, ffn2_norm), _tile_cols(ffn2_w_up, FFN_TF, ffn2_norm),
              _tile_cols(ffn2_w_down, FFN_TN))
    w_in_t = _tile_cols(jnp.concatenate(
        [w_in[:, :, :N_IN_BEFORE_PAD],
         jnp.zeros((depth, d, (BLK_DX * GROUP_W) - N_IN_BEFORE_PAD), w_in.dtype),
         w_in[:, :, N_IN_BEFORE_PAD:],
         jnp.zeros((depth, d, INPROJ_W - PROJ_W), w_in.dtype)], axis=2), INPROJ_TN, mix_norm)
    stream = (x2, *_stream_prep(x2))
    for l in range(depth):
        stream = _ffn(stream, *ffn1_w, l)
        x2 = stream[0]
        proj4 = _inproj(stream, w_in_t, l).reshape(INPROJ_W // LANES, bsz, seq, LANES)
        y_a = _hgrn(proj4, lb_logits, row(hgrn_norm[l]), layer=l).reshape(t, GROUP_W)
        y_b = _attn(proj4).reshape(t, GROUP_W)
        y_c = _ssd(proj4, ssm_conv_w[l].astype(F32), row(ssm_conv_b[l]), _pad_lanes(ssm_dt_bias[l].astype(F32)),
                   _pad_lanes(ssm_a_log[l].astype(F32)), row(jnp.repeat(ssm_d[l], SSM_HEAD_DIM)),
                   row(ssm_norm[l])).reshape(t, GROUP_W)
        y_d = _lru(proj4, lru_conv_w[l].astype(F32), row(lru_conv_b[l]), _block_diag(lru_w_a[l]).astype(BF16),
                   row(lru_b_a[l]), _block_diag(lru_w_x[l]).astype(BF16), row(lru_b_x[l]), row(lru_a_param[l]),
                   row(lru_norm[l])).reshape(t, GROUP_W)
        stream = _outproj(x2, y_a, y_b, y_c, y_d, row(attn_norm[l]), w_out[l].astype(BF16))
        stream = _ffn(stream, *ffn2_w, l)
    return _final_norm(stream[0], row(final_norm)).reshape(bsz, seq, d)
```

```python
import functools

import numpy as np
import jax
import jax.numpy as jnp
from jax import lax
from jax.experimental import pallas as pl
from jax.experimental.pallas import tpu as pltpu

F32 = jnp.float32
BF16 = jnp.bfloat16

NORM_EPS = 1e-6
MASK_VALUE = -1e30
GROUP_W = 512
HG_HEADS = 4
HG_HEAD_W = GROUP_W // HG_HEADS
ATT_HEAD_DIM = 64
ATT_SPAN = 128
ATT_DILATIONS = (1, 4, 16)
SSM_HEADS = 8
SSM_HEAD_DIM = 64
SSM_GROUPS = 2
SSM_STATE = 128
SSM_CONV_DIM = GROUP_W + 2 * SSM_GROUPS * SSM_STATE
CONV_W = 4
LRU_C = 8.0

LANES = 128
SUBLANES = 8
VMEM_LIMIT = 60 * 1024 * 1024

PROJ_BLOCKS = 13
PROJ_W = PROJ_BLOCKS * GROUP_W
BLK_A = 0
BLK_B = 4
BLK_CZ = 7
BLK_CXBC = 8
BLK_CDT = 10
BLK_DX = 11
BLK_DG = 12
N_IN_BEFORE_PAD = 10 * GROUP_W + SSM_HEADS
SLABS = GROUP_W // LANES

CHUNK = 256
CHUNK_LEVELS = 8
STEP_CHUNKS = 4
STEP_ROWS = STEP_CHUNKS * CHUNK


def _cparams(*sem):
    return pltpu.CompilerParams(dimension_semantics=sem, vmem_limit_bytes=VMEM_LIMIT)


def _rms(x, w):
    return x * lax.rsqrt(jnp.mean(x * x, axis=-1, keepdims=True) + NORM_EPS) * w


def _silu(x):
    return x * jax.nn.sigmoid(x)


def _softplus(x):
    return jnp.maximum(x, 0.0) + jnp.log1p(jnp.exp(-jnp.abs(x)))


def _expm1(u):
    w = jnp.exp(u)
    near = (jnp.abs(u) < 0.5) & (w != 1.0)
    kahan = (w - 1.0) * u / jnp.log(jnp.where(near, w, 2.0))
    return jnp.where(near, kahan, jnp.where(w == 1.0, u, w - 1.0))


def _dot(a, b):
    return jnp.dot(a, b, preferred_element_type=F32)


def _dot_nt(a, b):
    return lax.dot_general(a, b, (((1,), (1,)), ((), ())), preferred_element_type=F32)


def _dot_tn(a, b):
    return lax.dot_general(a, b, (((0,), (0,)), ((), ())), preferred_element_type=F32)


def _sum01_matmul(m01, x):
    hi = x.astype(BF16)
    r1 = x - hi.astype(F32)
    mid = r1.astype(BF16)
    lo = (r1 - mid.astype(F32)).astype(BF16)
    return _dot(m01, hi) + _dot(m01, mid) + _dot(m01, lo)


def _sum01_matmul2(m01, x):
    hi = x.astype(BF16)
    mid = (x - hi.astype(F32)).astype(BF16)
    both = _dot(m01, jnp.concatenate([hi, mid], axis=1))
    n = x.shape[1]
    return both[:, :n] + both[:, n:]


def _row_scale(r, width):
    return jnp.concatenate([r] * (width // LANES), axis=1)


def _rinv(ssq, d):
    return lax.rsqrt(ssq * (1.0 / d) + NORM_EPS)


def _stream_prep_kernel(x_ref, xb_ref, r_ref):
    x = x_ref[...]
    xb_ref[...] = x.astype(BF16)
    r_ref[...] = jnp.broadcast_to(_rinv(jnp.sum(x * x, axis=-1, keepdims=True), x.shape[1]), r_ref.shape)


def _stream_prep(x2, *, tm=1024):
    t, d = x2.shape
    return pl.pallas_call(
        _stream_prep_kernel,
        grid=(t // tm,),
        in_specs=[pl.BlockSpec((tm, d), lambda i: (i, 0))],
        out_specs=[pl.BlockSpec((tm, d), lambda i: (i, 0)), pl.BlockSpec((tm, LANES), lambda i: (i, 0))],
        out_shape=[jax.ShapeDtypeStruct((t, d), BF16), jax.ShapeDtypeStruct((t, LANES), F32)],
        compiler_params=_cparams("parallel"),
        name="stream_prep",
    )(x2)


def _ffn_kernel(xb_ref, r_ref, xc_ref, wg_ref, wu_ref, wd_ref, o_ref, ob_ref, ro_ref, a_ref, ssq_ref,
                *, n_f, tf, d):
    j = pl.program_id(1)

    @pl.when(j < n_f)
    def _():
        xb = xb_ref[...]
        rt = _row_scale(r_ref[...], tf)
        a_ref[j] = (_silu(_dot(xb, wg_ref[...]) * rt) * (_dot(xb, wu_ref[...]) * rt)).astype(BF16)

    @pl.when(j >= n_f)
    def _():
        acc = _dot(a_ref[0], wd_ref[0:tf, :])
        for jj in range(1, n_f):
            acc = acc + _dot(a_ref[jj], wd_ref[jj * tf:(jj + 1) * tf, :])
        y = xc_ref[...] + 0.5 * acc
        o_ref[...] = y
        ob_ref[...] = y.astype(BF16)
        part = jnp.broadcast_to(jnp.sum(y * y, axis=-1, keepdims=True), ssq_ref.shape)
        ssq = part + jnp.where(j == n_f, 0.0, ssq_ref[...])
        ssq_ref[...] = ssq
        ro_ref[...] = _rinv(ssq, d)


FFN_TM = 1024
FFN_TF = 512
FFN_TN = 512


def _tile_cols_kernel(w_ref, g_ref, o_ref):
    o_ref[...] = (w_ref[...] * g_ref[...]).astype(BF16)


def _tile_cols(w, tile, gain=None):
    nl, k, n = w.shape
    kb = k // 2
    g = jnp.ones((nl, k), F32) if gain is None else gain.astype(F32)
    return pl.pallas_call(
        _tile_cols_kernel,
        grid=(nl, n // tile, k // kb),
        in_specs=[pl.BlockSpec((None, kb, tile), lambda l, j, kk: (l, kk, j)),
                  pl.BlockSpec((None, kb, 1), lambda l, j, kk: (l, kk, 0))],
        out_specs=pl.BlockSpec((None, None, kb, tile), lambda l, j, kk: (l, j, kk, 0)),
        out_shape=jax.ShapeDtypeStruct((nl, n // tile, k, tile), BF16),
        compiler_params=_cparams("parallel", "parallel", "parallel"),
        name="tile_cols",
    )(w.astype(F32), g.reshape(nl, k, 1))


def _ffn(stream, wg_t, wu_t, wd_t, layer, *, tm=FFN_TM):
    x2, xb, r = stream
    t, d = x2.shape
    _, n_f, _, tf = wg_t.shape
    _, n_d, f, tn = wd_t.shape
    up = lambda i, j: (layer, jnp.minimum(j, n_f - 1), 0, 0)
    down = lambda i, j: (layer, jnp.maximum(j - n_f, 0), 0, 0)
    col = lambda i, j: (i, jnp.maximum(j - n_f, 0))
    rows = lambda i, j: (i, 0)
    return pl.pallas_call(
        functools.partial(_ffn_kernel, n_f=n_f, tf=tf, d=d),
        grid=(t // tm, n_f + n_d),
        in_specs=[
            pl.BlockSpec((tm, d), rows),
            pl.BlockSpec((tm, LANES), rows),
            pl.BlockSpec((tm, tn), col),
            pl.BlockSpec((None, None, d, tf), up),
            pl.BlockSpec((None, None, d, tf), up),
            pl.BlockSpec((None, None, f, tn), down),
        ],
        out_specs=[pl.BlockSpec((tm, tn), col), pl.BlockSpec((tm, tn), col), pl.BlockSpec((tm, LANES), rows)],
        out_shape=[jax.ShapeDtypeStruct((t, d), F32), jax.ShapeDtypeStruct((t, d), BF16),
                   jax.ShapeDtypeStruct((t, LANES), F32)],
        scratch_shapes=[pltpu.VMEM((n_f, tm, tf), BF16), pltpu.VMEM((tm, LANES), F32)],
        compiler_params=_cparams("parallel", "arbitrary"),
        name="ffn",
    )(xb, r, x2, wg_t, wu_t, wd_t)


def _final_norm_kernel(x_ref, w_ref, o_ref):
    o_ref[...] = _rms(x_ref[...], w_ref[...])


def _final_norm(x2, w, *, tm=1024):
    t, d = x2.shape
    return pl.pallas_call(
        _final_norm_kernel,
        grid=(t // tm,),
        in_specs=[pl.BlockSpec((tm, d), lambda i: (i, 0)), pl.BlockSpec((1, d), lambda i: (0, 0))],
        out_specs=pl.BlockSpec((tm, d), lambda i: (i, 0)),
        out_shape=jax.ShapeDtypeStruct((t, d), F32),
        compiler_params=_cparams("parallel"),
        name="final_norm",
    )(x2, w)


def _inproj_kernel(xb_ref, r_ref, w_ref, o_ref):
    r = r_ref[...]
    res = _dot(xb_ref[...], w_ref[...])
    for p in range(o_ref.shape[0]):
        o_ref[p] = res[:, p * LANES:(p + 1) * LANES] * r


INPROJ_TN = 2304
INPROJ_W = 3 * INPROJ_TN
assert INPROJ_W >= PROJ_W and INPROJ_TN % LANES == 0


def _inproj(stream, w_t, layer, *, tm=1024):
    _, xb, r = stream
    t, d = xb.shape
    _, n_t, _, tn = w_t.shape
    return pl.pallas_call(
        _inproj_kernel,
        grid=(t // tm, n_t),
        in_specs=[
            pl.BlockSpec((tm, d), lambda i, j: (i, 0)),
            pl.BlockSpec((tm, LANES), lambda i, j: (i, 0)),
            pl.BlockSpec((None, None, d, tn), lambda i, j: (layer, j, 0, 0)),
        ],
        out_specs=pl.BlockSpec((tn // LANES, tm, LANES), lambda i, j: (j, i, 0)),
        out_shape=jax.ShapeDtypeStruct((n_t * tn // LANES, t, LANES), F32),
        compiler_params=_cparams("parallel", "arbitrary"),
        name="inproj",
    )(xb, r, w_t)


def _outproj_kernel(x_ref, ya_ref, yb_ref, yc_ref, yd_ref, bnw_ref, w_ref, o_ref, ob_ref, ro_ref):
    yb = _rms(yb_ref[...], bnw_ref[...]).astype(BF16)
    acc = x_ref[...]
    for g, y in enumerate((ya_ref[...], yb, yc_ref[...], yd_ref[...])):
        acc = acc + _dot(y, w_ref[g * GROUP_W:(g + 1) * GROUP_W, :])
    o_ref[...] = acc
    ob_ref[...] = acc.astype(BF16)
    ro_ref[...] = jnp.broadcast_to(_rinv(jnp.sum(acc * acc, axis=-1, keepdims=True), acc.shape[1]), ro_ref.shape)


def _outproj(x2, ya, yb, yc, yd, attn_norm_w, w, *, tm=512):
    t, d = x2.shape
    yspec = pl.BlockSpec((tm, GROUP_W), lambda i: (i, 0))
    rows = pl.BlockSpec((tm, d), lambda i: (i, 0))
    return pl.pallas_call(
        _outproj_kernel,
        grid=(t // tm,),
        in_specs=[rows, yspec, yspec, yspec, yspec,
                  pl.BlockSpec((1, GROUP_W), lambda i: (0, 0)), pl.BlockSpec((4 * GROUP_W, d), lambda i: (0, 0))],
        out_specs=[rows, rows, pl.BlockSpec((tm, LANES), lambda i: (i, 0))],
        out_shape=[jax.ShapeDtypeStruct((t, d), F32), jax.ShapeDtypeStruct((t, d), BF16),
                   jax.ShapeDtypeStruct((t, LANES), F32)],
        compiler_params=_cparams("parallel"),
        name="outproj",
    )(x2, ya, yb, yc, yd, attn_norm_w, w)


HG_LOW_LEVELS = 3


def _hgrn_tables(c=CHUNK, levels=HG_LOW_LEVELS):
    t = np.arange(c)
    rows = []
    for lvl in range(levels):
        h = 1 << lvl
        r = (t // (2 * h)) * (2 * h) + h
        up = (t // h) % 2 == 1
        u = t[None, :]
        m = np.where(up[:, None], (u > r[:, None]) & (u <= t[:, None]), (u > t[:, None]) & (u <= r[:, None]))
        rows.append(m)
    rows.append(t[None, :] <= t[:, None])
    mstack = np.concatenate(rows, axis=0).astype(np.float32)
    x = t[:, None] ^ t[None, :]
    lv = np.where(t[None, :] < t[:, None], np.floor(np.log2(np.maximum(x, 1))).astype(np.int32), -1)
    return mstack, lv.astype(np.int32)


_HGRN_MSTACK, _HGRN_LEVEL = _hgrn_tables()


def _hgrn_kernel(q_ref, f_ref, i_ref, g_ref, lbl_ref, nw_ref, mst_ref, lv_ref, o_ref, st_ref, *, layer):
    c = CHUNK

    @pl.when(pl.program_id(1) == 0)
    def _():
        st_ref[...] = jnp.zeros_like(st_ref)

    lbl = lbl_ref[...]
    e = jnp.exp(lbl - jnp.max(lbl, axis=0, keepdims=True))
    p = e / jnp.sum(e, axis=0, keepdims=True)
    lb = jnp.zeros((1, GROUP_W), F32)
    for i in range(1, layer + 1):
        lb = lb + p[i:i + 1, :]

    lax.fori_loop(0, STEP_CHUNKS, functools.partial(
        _hgrn_chunk, q_ref, f_ref, i_ref, g_ref, nw_ref, mst_ref, lv_ref, o_ref, st_ref, lb), 0)


def _hgrn_chunk(q_ref, f_ref, i_ref, g_ref, nw_ref, mst_ref, lv_ref, o_ref, st_ref, lb, ci, carry):
    c = CHUNK
    rows = pl.ds(pl.multiple_of(ci * c, c), c)
    low = HG_LOW_LEVELS
    mst_low = mst_ref[0:low * c, :]
    tril01 = mst_ref[low * c:(low + 1) * c, :]
    lv = lv_ref[...]
    for h in range(HG_HEADS):
        sl = slice(h * HG_HEAD_W, (h + 1) * HG_HEAD_W)
        q = _silu(q_ref[h, rows, :])
        z = f_ref[h, rows, :]
        v = i_ref[h, rows, :]
        lbh = lb[:, sl]
        g = jnp.log(lbh + (1.0 - lbh) * jax.nn.sigmoid(z))
        k = (1.0 - lbh) * jax.nn.sigmoid(-z)
        ex_low = jnp.exp(_sum01_matmul2(mst_low, g))
        b = _sum01_matmul(tril01, g)
        a = jnp.zeros((c, c), F32)
        for lvl in range(CHUNK_LEVELS):
            if lvl < low:
                exl = ex_low[lvl * c:(lvl + 1) * c]
            else:
                hh = 1 << lvl
                b_r = jnp.concatenate(
                    [jnp.broadcast_to(b[kb * 2 * hh + hh:kb * 2 * hh + hh + 1, :], (2 * hh, HG_HEAD_W))
                     for kb in range(c // (2 * hh))], axis=0)
                exl = jnp.exp(-jnp.abs(b - b_r))
            a = jnp.where(lv == lvl, _dot_nt((q * exl).astype(BF16), (k * exl).astype(BF16)), a)
        vb = v.astype(BF16)
        ex_b = jnp.exp(b)
        ex_r = jnp.exp(b[c - 1:c, :] - b)
        st = st_ref[h]
        o = (_dot(a.astype(BF16), vb) + jnp.sum(q * k, axis=-1, keepdims=True) * v
             + _dot_nt((q * ex_b).astype(BF16), st.astype(BF16)))
        st_ref[h] = ex_b[c - 1:c, :] * st + _dot_tn(vb, (k * ex_r).astype(BF16))
        o = _rms(o, nw_ref[:, sl]) * _silu(g_ref[h, rows, :])
        o_ref[rows, sl] = o.astype(o_ref.dtype)
    return carry


def _slab_spec(blk, rows, n=SLABS):
    first = blk * SLABS // n
    return pl.BlockSpec((n, None, rows, LANES), lambda bi, ci: (first, bi, ci, 0))


def _hgrn(proj4, lb_logits, norm_w, *, layer):
    _, b, s, _ = proj4.shape
    cspec = lambda blk: _slab_spec(blk, STEP_ROWS)
    full = lambda shape: pl.BlockSpec(shape, lambda bi, ci: (0,) * len(shape))
    mst = jnp.asarray(_HGRN_MSTACK, BF16)
    lv = jnp.asarray(_HGRN_LEVEL)
    return pl.pallas_call(
        functools.partial(_hgrn_kernel, layer=layer),
        grid=(b, s // STEP_ROWS),
        in_specs=[cspec(BLK_A), cspec(BLK_A + 1), cspec(BLK_A + 2), cspec(BLK_A + 3),
                  full(lb_logits.shape), full((1, GROUP_W)), full(mst.shape), full(lv.shape)],
        out_specs=pl.BlockSpec((None, STEP_ROWS, GROUP_W), lambda bi, ci: (bi, ci, 0)),
        out_shape=jax.ShapeDtypeStruct((b, s, GROUP_W), BF16),
        scratch_shapes=[pltpu.VMEM((HG_HEADS, HG_HEAD_W, HG_HEAD_W), F32)],
        compiler_params=_cparams("parallel", "arbitrary"),
        name="hgrn2",
    )(proj4, proj4, proj4, proj4, lb_logits, norm_w, mst, lv)


ATT_TILE = 2048
ATT_GROUP = 16
ATT_MID = 4
assert ATT_DILATIONS == (1, ATT_MID, ATT_MID * ATT_MID) and ATT_TILE == ATT_SPAN * ATT_DILATIONS[-1]


def _attend_group(qs, ks, vs, first_flags, bias_all, bias_first, first_head):
    blk = ATT_SPAN
    scores = []
    for q2, k2, ff in zip(qs, ks, first_flags):
        qq = jnp.concatenate([jnp.where(first_head, q2, 0.0), jnp.where(first_head, 0.0, q2)], axis=0).astype(BF16)
        bias = jnp.where(ff > 0, bias_first, bias_all)
        scores.append(_dot_nt(qq, k2) + jnp.concatenate([bias, bias], axis=0))
    s = jnp.concatenate(scores, axis=0)
    m = jnp.max(s, axis=-1, keepdims=True)
    p = jnp.exp(s - m)
    den = jnp.sum(p, axis=-1, keepdims=True)
    lse = m + jnp.log(den)
    pb = p.astype(BF16)
    outs = []
    for g, v2 in enumerate(vs):
        r0 = g * 2 * blk
        pv = _dot(pb[r0:r0 + 2 * blk], v2)
        d0, d1 = den[r0:r0 + blk], den[r0 + blk:r0 + 2 * blk]
        l0, l1 = lse[r0:r0 + blk], lse[r0 + blk:r0 + 2 * blk]
        o2 = jnp.where(first_head, pv[:blk], pv[blk:]) / jnp.where(first_head, d0, d1)
        outs.append((o2, jnp.where(first_head, l0, l1)))
    return outs


def _attn_kernel(q_ref, k_ref, v_ref, y_ref, q4_ref, k4_ref, v4_ref, kd1, vd1, kd4, vd4, kd16, vd16, op_ref, lp_ref):
    i = pl.program_id(2)
    blk, tile, mid = ATT_SPAN, ATT_TILE, ATT_MID
    len4 = tile // mid
    pitch4 = blk + len4
    pitch16 = 2 * blk
    scale = ATT_HEAD_DIM ** -0.5
    zeros = jnp.zeros((blk, LANES), BF16)

    @pl.when(i == 0)
    def _():
        for kd, vd, pitch, nres in ((kd1, vd1, 0, 1), (kd4, vd4, pitch4, mid), (kd16, vd16, pitch16, mid * mid)):
            for r in range(nres):
                kd[r * pitch:r * pitch + blk, :] = zeros
                vd[r * pitch:r * pitch + blk, :] = zeros

    kd1[blk:blk + tile, :] = k_ref[...].astype(BF16)
    vd1[blk:blk + tile, :] = v_ref[...].astype(BF16)
    for r in range(mid):
        rows = slice(r * len4, (r + 1) * len4)
        cur = slice(r * pitch4 + blk, (r + 1) * pitch4)
        kk = k_ref[pl.ds(r, len4, stride=mid), :]
        vv = v_ref[pl.ds(r, len4, stride=mid), :]
        k4_ref[rows, :] = kk
        v4_ref[rows, :] = vv
        kd4[cur, :] = kk.astype(BF16)
        vd4[cur, :] = vv.astype(BF16)
        q4_ref[rows, :] = q_ref[pl.ds(r, len4, stride=mid), :] * scale
    for r4 in range(mid):
        for hi in range(mid):
            r16 = r4 + mid * hi
            cur = slice(r16 * pitch16 + blk, (r16 + 1) * pitch16)
            kd16[cur, :] = k4_ref[pl.ds(r4 * len4 + hi, blk, stride=mid), :].astype(BF16)
            vd16[cur, :] = v4_ref[pl.ds(r4 * len4 + hi, blk, stride=mid), :].astype(BF16)

    iq = lax.broadcasted_iota(jnp.int32, (blk, 2 * blk), 0)
    ik = lax.broadcasted_iota(jnp.int32, (blk, 2 * blk), 1)
    dist = iq + blk - ik
    valid = (dist >= 0) & (dist <= blk)
    bias_all = jnp.where(valid, 0.0, MASK_VALUE)
    bias_first = jnp.where(valid & (ik >= blk), 0.0, MASK_VALUE)
    lane = lax.broadcasted_iota(jnp.int32, (1, LANES), 1)
    first_head = lane < ATT_HEAD_DIM
    tile0 = 1 - jnp.minimum(i, 1)
    grp = ATT_GROUP

    def attend(qs, kd, vd, bases, flags):
        ks = [kd[pl.ds(pl.multiple_of(b0, blk), 2 * blk), :] for b0 in bases]
        vs = [vd[pl.ds(pl.multiple_of(b0, blk), 2 * blk), :] for b0 in bases]
        return _attend_group(qs, ks, vs, flags, bias_all, bias_first, first_head)

    def pattern1(it, carry):
        b0s = [(it * grp + g) * blk for g in range(grp)]
        qs = [q_ref[pl.ds(pl.multiple_of(b0, blk), blk), :] * scale for b0 in b0s]
        flags = [tile0 * (1 - jnp.minimum(it * grp + g, 1)) for g in range(grp)]
        for b0, (o2, l2) in zip(b0s, attend(qs, kd1, vd1, b0s, flags)):
            op_ref[0, pl.ds(pl.multiple_of(b0, blk), blk), :] = o2
            lp_ref[0, pl.ds(pl.multiple_of(b0, blk), blk), :] = l2
        return carry

    per_it = grp // mid

    def pattern4(it, carry):
        res = [it * per_it + g // mid for g in range(grp)]
        bks = [g % mid for g in range(grp)]
        qs = [q4_ref[pl.ds(pl.multiple_of(r * len4 + b * blk, blk), blk), :] for r, b in zip(res, bks)]
        bases = [r * pitch4 + b * blk for r, b in zip(res, bks)]
        flags = [tile0 if b == 0 else 0 * tile0 for b in bks]
        for r, b, (o2, l2) in zip(res, bks, attend(qs, kd4, vd4, bases, flags)):
            op_ref[1, pl.ds(r + mid * blk * b, blk, stride=mid), :] = o2
            lp_ref[1, pl.ds(r + mid * blk * b, blk, stride=mid), :] = l2
        return carry

    def pattern16(it, carry):
        r4s = [g % mid for g in range(grp)]
        his = [it * per_it + g // mid for g in range(grp)]
        qs = [q4_ref[pl.ds(r4 * len4 + hi, blk, stride=mid), :] for r4, hi in zip(r4s, his)]
        bases = [(r4 + mid * hi) * pitch16 for r4, hi in zip(r4s, his)]
        flags = [tile0] * grp
        for r4, hi, (o2, l2) in zip(r4s, his, attend(qs, kd16, vd16, bases, flags)):
            op_ref[2, pl.ds(r4 + mid * hi, blk, stride=mid * mid), :] = o2
            lp_ref[2, pl.ds(r4 + mid * hi, blk, stride=mid * mid), :] = l2
        return carry

    n_it = tile // blk // grp
    assert grp % mid == 0 and n_it * grp == tile // blk == mid * mid
    lax.fori_loop(0, n_it, pattern1, 0)
    lax.fori_loop(0, n_it, pattern4, 0)
    lax.fori_loop(0, n_it, pattern16, 0)

    for kd, vd, pitch, nres, ln in ((kd1, vd1, 0, 1, tile), (kd4, vd4, pitch4, mid, len4),
                                    (kd16, vd16, pitch16, mid * mid, blk)):
        for r in range(nres):
            kd[r * pitch:r * pitch + blk, :] = kd[r * pitch + ln:r * pitch + ln + blk, :]
            vd[r * pitch:r * pitch + blk, :] = vd[r * pitch + ln:r * pitch + ln + blk, :]

    def merge(c, carry):
        rows = pl.ds(pl.multiple_of(c * CHUNK, CHUNK), CHUNK)
        la, lb, lc = lp_ref[0, rows, :], lp_ref[1, rows, :], lp_ref[2, rows, :]
        m = jnp.maximum(jnp.maximum(la, lb), lc)
        wa, wb, wc = jnp.exp(la - m), jnp.exp(lb - m), jnp.exp(lc - m)
        y_ref[rows, :] = (wa * op_ref[0, rows, :] + wb * op_ref[1, rows, :] + wc * op_ref[2, rows, :]) / (wa + wb + wc)
        return carry

    lax.fori_loop(0, tile // CHUNK, merge, 0)


def _attn(proj4):
    _, b, s, _ = proj4.shape
    tile, blk, mid = ATT_TILE, ATT_SPAN, ATT_MID
    spec = lambda off: pl.BlockSpec((None, None, tile, LANES),
                                    lambda bi, pr, ti: ((BLK_B + off) * SLABS + pr, bi, ti, 0))
    return pl.pallas_call(
        _attn_kernel,
        grid=(b, SLABS, s // tile),
        in_specs=[spec(0), spec(1), spec(2)],
        out_specs=pl.BlockSpec((None, tile, LANES), lambda bi, pr, ti: (bi, ti, pr)),
        out_shape=jax.ShapeDtypeStruct((b, s, GROUP_W), F32),
        scratch_shapes=[pltpu.VMEM((tile, LANES), F32)] * 3
        + [pltpu.VMEM((blk + tile, LANES), BF16)] * 2
        + [pltpu.VMEM((mid * (blk + tile // mid), LANES), BF16)] * 2
        + [pltpu.VMEM((mid * mid * 2 * blk, LANES), BF16)] * 2
        + [pltpu.VMEM((len(ATT_DILATIONS), tile, LANES), F32)] * 2,
        compiler_params=_cparams("parallel", "parallel", "arbitrary"),
        name="dilattn",
    )(proj4, proj4, proj4)


def _causal_conv(x_ref, rows, xpad_ref, w_ref, b_ref):
    c = CHUNK
    for p in range(x_ref.shape[0]):
        xpad_ref[SUBLANES:SUBLANES + c, p * LANES:(p + 1) * LANES] = x_ref[p, rows, :]
    acc = b_ref[...] + w_ref[0:1, :] * xpad_ref[pl.ds(SUBLANES - CONV_W + 1, c), :]
    for j in range(1, CONV_W):
        acc = acc + w_ref[j:j + 1, :] * xpad_ref[pl.ds(SUBLANES - CONV_W + 1 + j, c), :]
    xpad_ref[0:SUBLANES, :] = xpad_ref[c:c + SUBLANES, :]
    return acc


def _pair_lanes(col_fn, first_head):
    return jnp.where(first_head, col_fn(0), col_fn(1))


def _ssd_kernel(z_ref, xbc_ref, dt_ref, cw_ref, cb_ref, dtb_ref, alog_ref, dsk_ref, nw_ref, tril_ref,
                y_ref, xpad_ref, st_ref):
    @pl.when(pl.program_id(1) == 0)
    def _():
        st_ref[...] = jnp.zeros_like(st_ref)
        xpad_ref[0:SUBLANES, :] = jnp.zeros((SUBLANES, xpad_ref.shape[1]), F32)

    lax.fori_loop(0, STEP_CHUNKS, functools.partial(
        _ssd_chunk, z_ref, xbc_ref, dt_ref, cw_ref, cb_ref, dtb_ref, alog_ref, dsk_ref, nw_ref, tril_ref,
        y_ref, xpad_ref, st_ref), 0)


def _ssd_chunk(z_ref, xbc_ref, dt_ref, cw_ref, cb_ref, dtb_ref, alog_ref, dsk_ref, nw_ref, tril_ref,
               y_ref, xpad_ref, st_ref, ci, carry):
    c = CHUNK
    rows = pl.ds(pl.multiple_of(ci * c, c), c)
    xbc = _silu(_causal_conv(xbc_ref, rows, xpad_ref, cw_ref, cb_ref))
    dt = _softplus(dt_ref[rows, :] + dtb_ref[...])
    adt = dt * (-jnp.exp(alog_ref[...]))
    tril01 = tril_ref[...]
    acs = _sum01_matmul(tril01, adt)
    acs_t = acs.T
    e_cs = jnp.exp(acs)
    dt_rest = dt * jnp.exp(acs[c - 1:c, :] - acs)
    causal = tril01 > 0

    lane = lax.broadcasted_iota(jnp.int32, (1, LANES), 1)
    first_head = lane < SSM_HEAD_DIM
    gw = GROUP_W // SSM_GROUPS
    hpg = SSM_HEADS // SSM_GROUPS
    for g in range(SSM_GROUPS):
        bg = xbc[:, GROUP_W + g * SSM_STATE:GROUP_W + (g + 1) * SSM_STATE]
        cg = xbc[:, GROUP_W + (SSM_GROUPS + g) * SSM_STATE:GROUP_W + (SSM_GROUPS + g + 1) * SSM_STATE].astype(BF16)
        cb = jnp.where(causal, _dot_nt(cg, bg.astype(BF16)), 0.0)
        st = st_ref[g]
        y_off = _dot(cg, st.astype(BF16))
        ys, xds, dec_last = [], [], []
        for pr in range(hpg // 2):
            h0 = g * hpg + 2 * pr
            ls = slice(h0 * SSM_HEAD_DIM, (h0 + 2) * SSM_HEAD_DIM)
            xs = xbc[:, ls]
            xdt = (xs * _pair_lanes(lambda i: dt[:, h0 + i:h0 + i + 1], first_head)).astype(BF16)
            yd = []
            for i in range(2):
                hh = h0 + i
                seg = jnp.minimum(acs[:, hh:hh + 1] - acs_t[hh:hh + 1, :], 0.0)
                yd.append(_dot((cb * jnp.exp(seg)).astype(BF16), xdt))
            y_pair = (jnp.where(first_head, yd[0], yd[1])
                      + y_off[:, pr * LANES:(pr + 1) * LANES] * _pair_lanes(lambda i: e_cs[:, h0 + i:h0 + i + 1], first_head)
                      + dsk_ref[:, ls] * xs)
            ys.append(y_pair)
            xds.append((xs * _pair_lanes(lambda i: dt_rest[:, h0 + i:h0 + i + 1], first_head)).astype(BF16))
            dec_last.append(_pair_lanes(lambda i: e_cs[c - 1:c, h0 + i:h0 + i + 1], first_head))
        st_ref[g] = (jnp.concatenate(dec_last, axis=1) * st
                     + _dot_tn(bg.astype(BF16), jnp.concatenate(xds, axis=1)))
        zg = jnp.concatenate([z_ref[g * (gw // LANES) + p, rows, :] for p in range(gw // LANES)], axis=1)
        yg = jnp.concatenate(ys, axis=1) * _silu(zg)
        y_ref[rows, g * gw:(g + 1) * gw] = _rms(yg, nw_ref[:, g * gw:(g + 1) * gw]).astype(y_ref.dtype)
    return carry


def _ssd(proj4, conv_w, conv_b, dt_bias, a_log, d_skip, norm_w):
    _, b, s, _ = proj4.shape
    full = lambda shape: pl.BlockSpec(shape, lambda bi, ci: (0,) * len(shape))
    tril = jnp.asarray(np.tril(np.ones((CHUNK, CHUNK), np.float32)), BF16)
    return pl.pallas_call(
        _ssd_kernel,
        grid=(b, s // STEP_ROWS),
        in_specs=[
            _slab_spec(BLK_CZ, STEP_ROWS),
            _slab_spec(BLK_CXBC, STEP_ROWS, n=SSM_CONV_DIM // LANES),
            pl.BlockSpec((None, None, STEP_ROWS, LANES), lambda bi, ci: (BLK_CDT * SLABS, bi, ci, 0)),
            full((CONV_W, SSM_CONV_DIM)), full((1, SSM_CONV_DIM)), full((1, LANES)), full((1, LANES)),
            full((1, GROUP_W)), full((1, GROUP_W)), full((CHUNK, CHUNK)),
        ],
        out_specs=pl.BlockSpec((None, STEP_ROWS, GROUP_W), lambda bi, ci: (bi, ci, 0)),
        out_shape=jax.ShapeDtypeStruct((b, s, GROUP_W), BF16),
        scratch_shapes=[pltpu.VMEM((CHUNK + SUBLANES, SSM_CONV_DIM), F32),
                        pltpu.VMEM((SSM_GROUPS, SSM_STATE, GROUP_W // SSM_GROUPS), F32)],
        compiler_params=_cparams("parallel", "arbitrary"),
        name="ssd",
    )(proj4, proj4, proj4, conv_w, conv_b, dt_bias, a_log, d_skip, norm_w, tril)


def _gelu_tanh(x):
    return 0.5 * x * (1.0 + jnp.tanh(np.sqrt(2.0 / np.pi).astype(np.float32) * (x + 0.044715 * (x * x * x))))


def _lru_kernel(x_ref, g_ref, cw_ref, cb_ref, wa_ref, ba_ref, wx_ref, bx_ref, ap_ref, nw_ref,
                y_ref, xpad_ref, h_ref):
    @pl.when(pl.program_id(1) == 0)
    def _():
        h_ref[...] = jnp.zeros_like(h_ref)
        xpad_ref[0:SUBLANES, :] = jnp.zeros((SUBLANES, xpad_ref.shape[1]), F32)

    lax.fori_loop(0, STEP_CHUNKS, functools.partial(
        _lru_chunk, x_ref, g_ref, cw_ref, cb_ref, wa_ref, ba_ref, wx_ref, bx_ref, ap_ref, nw_ref,
        y_ref, xpad_ref, h_ref), 0)


def _lru_chunk(x_ref, g_ref, cw_ref, cb_ref, wa_ref, ba_ref, wx_ref, bx_ref, ap_ref, nw_ref,
               y_ref, xpad_ref, h_ref, ci, carry):
    c = CHUNK
    rows = pl.ds(pl.multiple_of(ci * c, c), c)
    xc = _causal_conv(x_ref, rows, xpad_ref, cw_ref, cb_ref)
    xb = xc.astype(BF16)
    r = jax.nn.sigmoid(_dot(xb, wa_ref[...]) + ba_ref[...])
    i = jax.nn.sigmoid(_dot(xb, wx_ref[...]) + bx_ref[...])
    log_a = -LRU_C * r * _softplus(-ap_ref[...])
    a = jnp.exp(log_a)
    bt = jnp.sqrt(jnp.maximum(-_expm1(2.0 * log_a), 0.0)) * (i * xc)
    in_group = lax.broadcasted_iota(jnp.int32, (c, 1), 0) & (SUBLANES - 1)
    d = 1
    while d < SUBLANES:
        keep = in_group >= d
        a_s = pltpu.roll(a, d, 0)
        b_s = pltpu.roll(bt, d, 0)
        bt = jnp.where(keep, a * b_s + bt, bt)
        a = jnp.where(keep, a * a_s, a)
        d *= 2
    h_last = h_ref[...]
    groups = []
    for r in range(c // SUBLANES):
        rows8 = slice(r * SUBLANES, (r + 1) * SUBLANES)
        hr = bt[rows8] + a[rows8] * h_last
        groups.append(hr)
        h_last = hr[SUBLANES - 1:SUBLANES, :]
    h = jnp.concatenate(groups, axis=0)
    h_ref[...] = h_last
    gate = jnp.concatenate([g_ref[p, rows, :] for p in range(SLABS)], axis=1)
    y_ref[rows, :] = _rms(h * _gelu_tanh(gate), nw_ref[...]).astype(y_ref.dtype)
    return carry


def _lru(proj4, conv_w, conv_b, wa_bd, b_a, wx_bd, b_x, a_param, norm_w):
    _, b, s, _ = proj4.shape
    full = lambda shape: pl.BlockSpec(shape, lambda bi, ci: (0,) * len(shape))
    vec = full((1, GROUP_W))
    return pl.pallas_call(
        _lru_kernel,
        grid=(b, s // STEP_ROWS),
        in_specs=[
            _slab_spec(BLK_DX, STEP_ROWS),
            _slab_spec(BLK_DG, STEP_ROWS),
            full((CONV_W, GROUP_W)), vec, full((GROUP_W, GROUP_W)), vec, full((GROUP_W, GROUP_W)), vec, vec, vec,
        ],
        out_specs=pl.BlockSpec((None, STEP_ROWS, GROUP_W), lambda bi, ci: (bi, ci, 0)),
        out_shape=jax.ShapeDtypeStruct((b, s, GROUP_W), BF16),
        scratch_shapes=[pltpu.VMEM((CHUNK + SUBLANES, GROUP_W), F32), pltpu.VMEM((1, GROUP_W), F32)],
        compiler_params=_cparams("parallel", "arbitrary"),
        name="rglru",
    )(proj4, proj4, conv_w, conv_b, wa_bd, b_a, wx_bd, b_x, a_param, norm_w)


def _block_diag(w):
    nb, n, _ = w.shape
    eye = jnp.eye(nb, dtype=w.dtype)
    return jnp.einsum("hij,hg->higj", w, eye).reshape(nb * n, nb * n)


def _pad_lanes(v, width=LANES):
    return jnp.pad(v, (0, width - v.shape[0])).reshape(1, width)


def kernel(x, ffn1_norm, ffn1_w_gate, ffn1_w_up, ffn1_w_down, mix_norm, w_in, w_out, hgrn_lb_logits, hgrn_norm, attn_norm, ssm_conv_w, ssm_conv_b, ssm_dt_bias, ssm_a_log, ssm_d, ssm_norm, lru_conv_w, lru_conv_b, lru_w_a, lru_b_a, lru_w_x, lru_b_x, lru_a_param, lru_norm, ffn2_norm, ffn2_w_gate, ffn2_w_up, ffn2_w_down, final_norm):
    bsz, seq, d = x.shape
    depth = w_in.shape[0]
    t = bsz * seq
    row = lambda v: v.reshape(1, -1).astype(F32)
    x2 = x.reshape(t, d)
    lb_logits = hgrn_lb_logits.astype(F32)
    ffn1_w = (_tile_cols(ffn1_w_gate, FFN_TF, ffn1_norm), _tile_cols(ffn1_w_up, FFN_TF, ffn1_norm),
              _tile_cols(ffn1_w_down, FFN_TN))
    ffn2_w = (_tile_cols(ffn2_w_gate, FFN_TF, ffn2_norm), _tile_cols(ffn2_w_up, FFN_TF, ffn2_norm),
              _tile_cols(ffn2_w_down, FFN_TN))
    w_in_t = _tile_cols(jnp.concatenate(
        [w_in[:, :, :N_IN_BEFORE_PAD],
         jnp.zeros((depth, d, (BLK_DX * GROUP_W) - N_IN_BEFORE_PAD), w_in.dtype),
         w_in[:, :, N_IN_BEFORE_PAD:],
         jnp.zeros((depth, d, INPROJ_W - PROJ_W), w_in.dtype)], axis=2), INPROJ_TN, mix_norm)
    stream = (x2, *_stream_prep(x2))
    for l in range(depth):
        stream = _ffn(stream, *ffn1_w, l)
        x2 = stream[0]
        proj4 = _inproj(stream, w_in_t, l).reshape(INPROJ_W // LANES, bsz, seq, LANES)
        y_a = _hgrn(proj4, lb_logits, row(hgrn_norm[l]), layer=l).reshape(t, GROUP_W)
        y_b = _attn(proj4).reshape(t, GROUP_W)
        y_c = _ssd(proj4, ssm_conv_w[l].astype(F32), row(ssm_conv_b[l]), _pad_lanes(ssm_dt_bias[l].astype(F32)),
                   _pad_lanes(ssm_a_log[l].astype(F32)), row(jnp.repeat(ssm_d[l], SSM_HEAD_DIM)),
                   row(ssm_norm[l])).reshape(t, GROUP_W)
        y_d = _lru(proj4, lru_conv_w[l].astype(F32), row(lru_conv_b[l]), _block_diag(lru_w_a[l]).astype(BF16),
                   row(lru_b_a[l]), _block_diag(lru_w_x[l]).astype(BF16), row(lru_b_x[l]), row(lru_a_param[l]),
                   row(lru_norm[l])).reshape(t, GROUP_W)
        stream = _outproj(x2, y_a, y_b, y_c, y_d, row(attn_norm[l]), w_out[l].astype(BF16))
        stream = _ffn(stream, *ffn2_w, l)
    return _final_norm(stream[0], row(final_norm)).reshape(bsz, seq, d)
```
